```python
import functools
import jax, jax.numpy as jnp
from jax import lax
import numpy as np

D_MODEL = 1024
BATCH = 16
SEQ = 2048
DEPTH = 2
DEC_BATCH = 128
DEC_SEQ = 4
PAST_LEN = 16384
PAGE_SIZE = 128

D_MIX = D_MODEL
H_A = 8
DN_A = 64
DR_A = 32
DV_A = 64
Q_LORA = 384
KV_LORA = 256
ROPE_BASE = 10000.0
Q_BLOCK = 128
MLA_SCALE = (DN_A + DR_A) ** -0.5
H_B = 4
DK_B = 64
DV_B = 64
HG_CHUNK = 64
C_CH = 256
CONV_W = 31
N_GROUPS = 4
E_PER_GROUP = 8
N_EXPERTS = N_GROUPS * E_PER_GROUP
TOP_K_IN = 2
D_FF_E = 512
MOE_BLOCK = 128
ALPHA = (2 * DEPTH) ** 0.25
BETA = (8 * DEPTH) ** -0.25
EPS = 1e-6

D_A_OUT = H_A * DV_A
D_B_OUT = H_B * DV_B
IN_SIZES = (Q_LORA, KV_LORA, DR_A, H_B * DK_B, H_B * DK_B, H_B * DV_B, H_B * DV_B, C_CH, C_CH)
IN_SPLITS = tuple(int(v) for v in np.cumsum(IN_SIZES)[:-1])
D_IN = sum(IN_SIZES)

kernel_name = 'hymba_mla_hgrn2_conformer_hiermoe_step'


def rms_norm(x, g):
    xf = x.astype(jnp.float32)
    y = xf * lax.rsqrt(jnp.mean(xf * xf, axis=-1, keepdims=True) + EPS)
    return (y * g.astype(jnp.float32)).astype(x.dtype)


def layer_norm(x, g, b):
    xf = x.astype(jnp.float32)
    mu = jnp.mean(xf, axis=-1, keepdims=True)
    var = jnp.mean(jnp.square(xf - mu), axis=-1, keepdims=True)
    y = (xf - mu) * lax.rsqrt(var + EPS) * g.astype(jnp.float32) + b.astype(jnp.float32)
    return y.astype(x.dtype)


def rope(x, pos):
    half = DR_A // 2
    inv = ROPE_BASE ** (-jnp.arange(half, dtype=jnp.float32) / half)
    ang = pos.astype(jnp.float32)[:, None] * inv[None, :]
    shape = (1, ang.shape[0]) + (1,) * (x.ndim - 3) + (half,)
    cos, sin = jnp.cos(ang).reshape(shape), jnp.sin(ang).reshape(shape)
    x1 = x[..., :half].astype(jnp.float32)
    x2 = x[..., half:].astype(jnp.float32)
    return jnp.concatenate([x1 * cos - x2 * sin, x1 * sin + x2 * cos], axis=-1).astype(x.dtype)


def mla_project(cq_raw, ckv_raw, kr_raw, pos, q_norm, w_uq, kv_norm, w_uk):
    b, t, _ = cq_raw.shape
    q = (rms_norm(cq_raw, q_norm) @ w_uq).reshape(b, t, H_A, DN_A + DR_A)
    q_nope = q[..., :DN_A]
    q_rope = rope(q[..., DN_A:], pos).transpose(0, 2, 1, 3)
    c_kv = rms_norm(ckv_raw, kv_norm)
    k_rope = rope(kr_raw, pos)
    q_lat = jnp.einsum('bthn,chn->bhtc', q_nope, w_uk)
    return q_lat, q_rope, c_kv, k_rope


def mla_scores(q_lat, q_rope, ckv, krope):
    s = jnp.einsum('bhqc,bkc->bhqk', q_lat, ckv) + jnp.einsum('bhqr,bkr->bhqk', q_rope, krope)
    return s.astype(jnp.float32) * MLA_SCALE


def mla_attend_prompt(q_lat, q_rope, ckv, krope):
    b, h, t, c = q_lat.shape
    nb = t // Q_BLOCK
    qb = q_lat.reshape(b, h, nb, Q_BLOCK, c).transpose(2, 0, 1, 3, 4)
    qrb = q_rope.reshape(b, h, nb, Q_BLOCK, DR_A).transpose(2, 0, 1, 3, 4)
    key_pos = jnp.arange(t)

    def one_block(args):
        ql, qr, i = args
        qpos = i * Q_BLOCK + jnp.arange(Q_BLOCK)
        mask = key_pos[None, :] <= qpos[:, None]
        s = jnp.where(mask, mla_scores(ql, qr, ckv, krope), -jnp.inf)
        p = jax.nn.softmax(s, axis=-1)
        return jnp.einsum('bhqk,bkc->bhqc', p.astype(ckv.dtype), ckv)

    o = lax.map(one_block, (qb, qrb, jnp.arange(nb)))
    return o.transpose(1, 2, 0, 3, 4).reshape(b, h, t, c)


def mla_attend_sample(q_lat, q_rope, ckv_new, kr_new, ckv_past, kr_past):
    t = q_lat.shape[2]
    past_len = ckv_past.shape[1]
    s_past = mla_scores(q_lat, q_rope, ckv_past, kr_past)
    causal = jnp.tril(jnp.ones((t, t), dtype=bool))
    s_new = jnp.where(causal, mla_scores(q_lat, q_rope, ckv_new, kr_new), -jnp.inf)
    p = jax.nn.softmax(jnp.concatenate([s_past, s_new], axis=-1), axis=-1)
    o = jnp.einsum('bhqk,bkc->bhqc', p[..., :past_len].astype(ckv_past.dtype), ckv_past)
    o = o + jnp.einsum('bhqk,bkc->bhqc', p[..., past_len:].astype(ckv_new.dtype), ckv_new)
    return o


def hgrn_lower_bounds(lb_logits):
    p = jax.nn.softmax(lb_logits.astype(jnp.float32), axis=0)
    cs = jnp.cumsum(p, axis=0)
    return cs - cs[0:1]


def hgrn_project(q_raw, f_raw, i_raw, lb):
    b, t, _ = q_raw.shape

    def heads(x, d):
        return x.reshape(b, t, H_B, d).transpose(0, 2, 1, 3).astype(jnp.float32)

    z = heads(f_raw, DK_B)
    lb_h = lb.reshape(H_B, 1, DK_B)
    logf = jnp.logaddexp(jnp.log(lb_h), jnp.log1p(-lb_h) + jax.nn.log_sigmoid(z))
    k = (1.0 - lb_h) * jax.nn.sigmoid(-z)
    return heads(q_raw, DK_B), logf, k, heads(i_raw, DV_B)


def hgrn_chunk(S0, q, logf, k, v):
    c = q.shape[2]
    A = jnp.cumsum(logf, axis=2)
    mask = jnp.tril(jnp.ones((c, c), dtype=bool))
    diff = A[:, :, :, None, :] - A[:, :, None, :, :]
    decay = jnp.exp(jnp.where(mask[:, :, None], diff, -jnp.inf))
    attn = jnp.einsum('bhtk,bhtsk,bhsk->bhts', q, decay, k)
    o = jnp.einsum('bhts,bhsv->bhtv', attn, v) + jnp.einsum('bhtk,bhkv->bhtv', q * jnp.exp(A), S0)
    A_last = A[:, :, -1]
    S_new = jnp.exp(A_last)[..., None] * S0 + jnp.einsum('bhsk,bhsv->bhkv', k * jnp.exp(A_last[:, :, None] - A), v)
    return S_new, o


def hgrn_scan_prompt(q, logf, k, v):
    b, h, t, _ = q.shape
    nc = t // HG_CHUNK

    def to_chunks(x):
        return x.reshape(b, h, nc, HG_CHUNK, x.shape[-1]).transpose(2, 0, 1, 3, 4)

    S0 = jnp.zeros((b, h, DK_B, DV_B), jnp.float32)
    S, o = lax.scan(lambda S, xs: hgrn_chunk(S, *xs), S0, (to_chunks(q), to_chunks(logf), to_chunks(k), to_chunks(v)))
    return S, o.transpose(1, 2, 0, 3, 4).reshape(b, h, t, DV_B)


def hgrn_readout(o, g_raw, norm_g):
    b, h, t, _ = o.shape
    o = rms_norm(o.transpose(0, 2, 1, 3), norm_g)
    g = g_raw.reshape(b, t, H_B, DV_B).astype(jnp.float32)
    return (o * jax.nn.silu(g)).reshape(b, t, D_B_OUT).astype(g_raw.dtype)


def conv_module(a, gate, buf, w, bias, ln_g, ln_b):
    glu = a * jax.nn.sigmoid(gate)
    full = jnp.concatenate([buf.astype(glu.dtype), glu], axis=1)
    y = lax.conv_general_dilated(full, w[:, None, :].astype(glu.dtype), window_strides=(1,), padding='VALID',
                                 dimension_numbers=('NWC', 'WIO', 'NWC'), feature_group_count=C_CH)
    y = jax.nn.silu(layer_norm(y + bias.astype(y.dtype), ln_g, ln_b))
    return y, full[:, -(CONV_W - 1):]


def hier_moe(x, rg_w, rg_b, re_w, re_b, w_gate, w_up, w_down):
    b, t, d = x.shape
    T = b * t
    xf = x.reshape(T, d)
    gl = (xf @ rg_w + rg_b).astype(jnp.float32)
    pg = jax.nn.softmax(gl, axis=-1)
    _, g_idx = lax.top_k(gl, 1)
    p_g = jnp.take_along_axis(pg, g_idx, axis=-1)
    el = (xf @ re_w + re_b).astype(jnp.float32).reshape(T, N_GROUPS, E_PER_GROUP)
    el_g = jnp.take_along_axis(el, g_idx[:, :, None], axis=1)[:, 0]
    top_v, top_i = lax.top_k(el_g, TOP_K_IN)
    wts = p_g * jax.nn.softmax(top_v, axis=-1)
    e = (g_idx * E_PER_GROUP + top_i).reshape(T * TOP_K_IN)
    A = T * TOP_K_IN
    counts = jnp.bincount(e, length=N_EXPERTS)
    padded = ((counts + MOE_BLOCK - 1) // MOE_BLOCK) * MOE_BLOCK
    seg_end = jnp.cumsum(padded)
    seg_start = seg_end - padded
    start_unpadded = jnp.cumsum(counts) - counts
    order = jnp.argsort(e)
    e_sorted = e[order]
    rank_sorted = jnp.arange(A) - start_unpadded[e_sorted]
    dest = jnp.zeros((A,), jnp.int32).at[order].set((seg_start[e_sorted] + rank_sorted).astype(jnp.int32))
    nb = (A + N_EXPERTS * (MOE_BLOCK - 1) + MOE_BLOCK - 1) // MOE_BLOCK
    rows = jnp.zeros((nb * MOE_BLOCK, d), x.dtype).at[dest].set(jnp.repeat(xf, TOP_K_IN, axis=0))
    block_expert = jnp.minimum(jnp.searchsorted(seg_end, jnp.arange(nb) * MOE_BLOCK, side='right'), N_EXPERTS - 1)

    def expert_block(args):
        xb, eb = args
        hdn = jax.nn.silu(xb @ w_gate[eb]) * (xb @ w_up[eb])
        return hdn @ w_down[eb]

    yb = lax.map(expert_block, (rows.reshape(nb, MOE_BLOCK, d), block_expert))
    y = yb.reshape(nb * MOE_BLOCK, d)[dest].reshape(T, TOP_K_IN, d)
    return jnp.einsum('tk,tkd->td', wts.astype(y.dtype), y).reshape(b, t, d)


def trunk_layer(h, pos, p, attend, recur, conv_buf):
    b, t, _ = h.shape
    u = jnp.split(h @ p['w_in'], IN_SPLITS, axis=-1)
    q_lat, q_rope, c_kv, k_rope = mla_project(u[0], u[1], u[2], pos, p['q_norm'], p['w_uq'], p['kv_norm'], p['w_uk'])
    o_lat = attend(q_lat, q_rope, c_kv, k_rope)
    o_a = jnp.einsum('bhtc,chv->bthv', o_lat, p['w_uv']).reshape(b, t, D_A_OUT)
    hq, hlogf, hk, hv = hgrn_project(u[3], u[4], u[5], p['lb'])
    S, o_rec = recur(hq, hlogf, hk, hv)
    o_b = hgrn_readout(o_rec, u[6], p['hgrn_norm'])
    o_c, new_buf = conv_module(u[7], u[8], conv_buf, p['conv_w'], p['conv_b'], p['conv_ln_g'], p['conv_ln_b'])
    mix = jnp.concatenate([o_a, o_b, o_c], axis=-1) @ p['w_out']
    h = layer_norm(ALPHA * h + mix, p['ln1_g'], p['ln1_b'])
    moe_out = hier_moe(h, p['rg_w'], p['rg_b'], p['re_w'], p['re_b'], p['w_gate'], p['w_up'], p['w_down'])
    h = layer_norm(ALPHA * h + moe_out, p['ln2_g'], p['ln2_b'])
    return h, c_kv, k_rope, S, new_buf


def setup_inputs(seed: int = 0) -> dict:
    key = jax.random.key(seed)
    ks = iter(jax.random.split(key, 48))

    def nrm(shape, scale):
        return jax.random.normal(next(ks), shape, jnp.float32) * scale

    n_pages = PAST_LEN // PAGE_SIZE
    n_used = DEC_BATCH * n_pages
    n_pool = n_used + n_used // 4
    perm = jax.random.permutation(next(ks), n_pool)
    page_table = perm[:n_used].reshape(DEC_BATCH, n_pages).astype(jnp.int32)
    return {
        'x_prompt': nrm((BATCH, SEQ, D_MODEL), 1.0),
        'x_sample': nrm((DEC_BATCH, DEC_SEQ, D_MODEL), 1.0),
        'cache_ckv': nrm((DEPTH, n_pool, PAGE_SIZE, KV_LORA), 1.0),
        'cache_krope': nrm((DEPTH, n_pool, PAGE_SIZE, DR_A), 1.0),
        'state_hgrn': nrm((DEPTH, DEC_BATCH, H_B, DK_B, DV_B), 0.5),
        'state_conv': nrm((DEPTH, DEC_BATCH, CONV_W - 1, C_CH), 0.5),
        'page_table': page_table,
        'w_in': nrm((DEPTH, D_MODEL, D_IN), D_MODEL ** -0.5),
        'mla_q_norm': 1.0 + nrm((DEPTH, Q_LORA), 0.02),
        'mla_w_uq': nrm((DEPTH, Q_LORA, H_A * (DN_A + DR_A)), Q_LORA ** -0.5),
        'mla_kv_norm': 1.0 + nrm((DEPTH, KV_LORA), 0.02),
        'mla_w_uk': nrm((DEPTH, KV_LORA, H_A, DN_A), KV_LORA ** -0.5),
        'mla_w_uv': nrm((DEPTH, KV_LORA, H_A, DV_A), KV_LORA ** -0.5),
        'hgrn_lb_logits': nrm((DEPTH, H_B * DK_B), 1.0),
        'hgrn_norm': 1.0 + nrm((DEPTH, DV_B), 0.02),
        'conv_w': nrm((DEPTH, CONV_W, C_CH), CONV_W ** -0.5),
        'conv_b': nrm((DEPTH, C_CH), 0.02),
        'conv_ln_g': 1.0 + nrm((DEPTH, C_CH), 0.02),
        'conv_ln_b': nrm((DEPTH, C_CH), 0.02),
        'w_out': nrm((DEPTH, D_MIX, D_MODEL), BETA * D_MIX ** -0.5),
        'ln1_g': 1.0 + nrm((DEPTH, D_MODEL), 0.02),
        'ln1_b': nrm((DEPTH, D_MODEL), 0.02),
        'ln2_g': 1.0 + nrm((DEPTH, D_MODEL), 0.02),
        'ln2_b': nrm((DEPTH, D_MODEL), 0.02),
        'router_g_w': nrm((DEPTH, D_MODEL, N_GROUPS), D_MODEL ** -0.5),
        'router_g_b': nrm((DEPTH, N_GROUPS), 0.01),
        'router_e_w': nrm((DEPTH, D_MODEL, N_EXPERTS), D_MODEL ** -0.5),
        'router_e_b': nrm((DEPTH, N_EXPERTS), 0.01),
        'exp_w_gate': nrm((DEPTH, N_EXPERTS, D_MODEL, D_FF_E), D_MODEL ** -0.5),
        'exp_w_up': nrm((DEPTH, N_EXPERTS, D_MODEL, D_FF_E), D_MODEL ** -0.5),
        'exp_w_down': nrm((DEPTH, N_EXPERTS, D_FF_E, D_MODEL), BETA * D_FF_E ** -0.5),
    }


def reference(x_prompt, x_sample, cache_ckv, cache_krope, state_hgrn, state_conv, page_table,
              w_in, mla_q_norm, mla_w_uq, mla_kv_norm, mla_w_uk, mla_w_uv, hgrn_lb_logits, hgrn_norm,
              conv_w, conv_b, conv_ln_g, conv_ln_b, w_out, ln1_g, ln1_b, ln2_g, ln2_b,
              router_g_w, router_g_b, router_e_w, router_e_b, exp_w_gate, exp_w_up, exp_w_down):
    lb_all = hgrn_lower_bounds(hgrn_lb_logits)
    dec_b = page_table.shape[0]
    past_len = page_table.shape[1] * cache_ckv.shape[2]
    pos_p = jnp.arange(x_prompt.shape[1])
    pos_s = past_len + jnp.arange(x_sample.shape[1])
    hp, hs = x_prompt, x_sample
    ckv_p, kr_p, sh_p, sc_p = [], [], [], []
    ckv_s, kr_s, sh_s, sc_s = [], [], [], []
    for l in range(DEPTH):
        p = {'w_in': w_in[l], 'q_norm': mla_q_norm[l], 'w_uq': mla_w_uq[l], 'kv_norm': mla_kv_norm[l],
             'w_uk': mla_w_uk[l], 'w_uv': mla_w_uv[l], 'lb': lb_all[l], 'hgrn_norm': hgrn_norm[l],
             'conv_w': conv_w[l], 'conv_b': conv_b[l], 'conv_ln_g': conv_ln_g[l], 'conv_ln_b': conv_ln_b[l],
             'w_out': w_out[l], 'ln1_g': ln1_g[l], 'ln1_b': ln1_b[l], 'ln2_g': ln2_g[l], 'ln2_b': ln2_b[l],
             'rg_w': router_g_w[l], 'rg_b': router_g_b[l], 're_w': router_e_w[l], 're_b': router_e_b[l],
             'w_gate': exp_w_gate[l], 'w_up': exp_w_up[l], 'w_down': exp_w_down[l]}
        buf0 = jnp.zeros((hp.shape[0], CONV_W - 1, C_CH), hp.dtype)
        hp, c1, k1, s1, b1 = trunk_layer(hp, pos_p, p, mla_attend_prompt, hgrn_scan_prompt, buf0)
        ckv_p.append(c1); kr_p.append(k1); sh_p.append(s1); sc_p.append(b1)
        ckv_past = cache_ckv[l][page_table].reshape(dec_b, past_len, KV_LORA)
        kr_past = cache_krope[l][page_table].reshape(dec_b, past_len, DR_A)
        attend_s = functools.partial(mla_attend_sample, ckv_past=ckv_past, kr_past=kr_past)
        recur_s = functools.partial(hgrn_chunk, state_hgrn[l].astype(jnp.float32))
        hs, c2, k2, s2, b2 = trunk_layer(hs, pos_s, p, attend_s, recur_s, state_conv[l])
        ckv_s.append(c2); kr_s.append(k2); sh_s.append(s2); sc_s.append(b2)
    return (hp, hs, jnp.stack(ckv_p), jnp.stack(kr_p), jnp.stack(sh_p), jnp.stack(sc_p),
            jnp.stack(ckv_s), jnp.stack(kr_s), jnp.stack(sh_s), jnp.stack(sc_s))
```

```python
import functools
import math

import jax
import jax.numpy as jnp
from jax import lax
from jax.experimental import pallas as pl
from jax.experimental.pallas import tpu as pltpu

F32 = jnp.float32
BF16 = jnp.bfloat16
I32 = jnp.int32

LANES = 128
ROPE_BASE = 10000.0
EPS = 1e-6
TM = 256
MOE_BM = 256
HGRN_C = 16
CONV_HIST = 32
VMEM_LIMIT = 48 * 1024 * 1024
_HI = lax.Precision.HIGHEST
_NT = (((1,), (1,)), ((), ()))
_TN = (((0,), (0,)), ((), ()))


def _cp(*sem):
    return pltpu.CompilerParams(dimension_semantics=sem, vmem_limit_bytes=VMEM_LIMIT)


def _dot(a, b):
    return jnp.dot(a, b, preferred_element_type=F32)


def _dot_hi(a, b):
    return jnp.dot(a, b, preferred_element_type=F32, precision=_HI)


def _rms(x, g):
    return x * lax.rsqrt(jnp.mean(x * x, axis=-1, keepdims=True) + EPS) * g


def _ln(x, g, b):
    mu = jnp.mean(x, axis=-1, keepdims=True)
    xc = x - mu
    var = jnp.mean(xc * xc, axis=-1, keepdims=True)
    return xc * lax.rsqrt(var + EPS) * g + b


def _full(shape):
    n = len(shape)
    return pl.BlockSpec(shape, lambda *_: (0,) * n)


def _proj_common(x, win_ref, qn_ref, wuq_ref, kvn_ref, cq_ref, sq_ref, ck_ref, sk_ref, lb_ref, *, ql, kv, heads):
    o_kr = ql + kv
    o_h = o_kr + LANES
    o_c = o_h + 4 * 256
    y = _rms(_dot(x, win_ref[:, 0:ql]), qn_ref[...]).astype(BF16)
    q = _dot(y, wuq_ref[...])
    cqt, sqt = cq_ref[...], sq_ref[...]
    qg = []
    for g in range(heads):
        t = q[:, g * LANES:(g + 1) * LANES]
        qg.append(t * cqt + pltpu.roll(t, 96, axis=1) * sqt)
    ckv = _rms(_dot(x, win_ref[:, ql:o_kr]), kvn_ref[...])
    kr = _dot(x, win_ref[:, o_kr:o_h])
    kr_rot = kr * ck_ref[...] + pltpu.roll(kr, 96, axis=1) * sk_ref[...]
    uh = _dot(x, win_ref[:, o_h:o_c])
    hq, z, hv, hg = uh[:, 0:256], uh[:, 256:512], uh[:, 512:768], uh[:, 768:1024]
    log_lb, log1m_lb, one_m_lb = lb_ref[0:1, :], lb_ref[1:2, :], lb_ref[2:3, :]
    log_sig = jnp.minimum(z, 0.0) - jnp.log1p(jnp.exp(-jnp.abs(z)))
    b = log1m_lb + log_sig
    hlf = jnp.maximum(log_lb, b) + jnp.log1p(jnp.exp(-jnp.abs(log_lb - b)))
    hk = one_m_lb * jax.nn.sigmoid(-z)
    uc = _dot(x, win_ref[:, o_c:o_c + 512])
    glu = uc[:, 0:256] * jax.nn.sigmoid(uc[:, 256:512])
    return qg, ckv, kr_rot, (hq, hlf, hk, hv, hg), glu


def _inproj_prompt_kernel(h_ref, win_ref, qn_ref, wuq_ref, kvn_ref, cq_ref, sq_ref, ck_ref, sk_ref, lb_ref,
                          wuk_ref, wuv_ref,
                          qatt_ref, katt_ref, vatt_ref, ckv_ref, kr_ref, hq_ref, hlf_ref, hk_ref, hv_ref, hg_ref,
                          glu_ref, *, ql, kv, heads):
    x = h_ref[...].astype(BF16)
    qg, ckv, kr_rot, hg5, glu = _proj_common(x, win_ref, qn_ref, wuq_ref, kvn_ref, cq_ref, sq_ref, ck_ref, sk_ref,
                                             lb_ref, ql=ql, kv=kv, heads=heads)
    cb = ckv.astype(BF16)
    kn = _dot(cb, wuk_ref[...])
    for g in range(heads):
        sl = slice(g * LANES, (g + 1) * LANES)
        qatt_ref[:, sl] = qg[g].astype(BF16)
        katt_ref[:, sl] = (kn[:, sl] + kr_rot).astype(BF16)
    vatt_ref[...] = _dot(cb, wuv_ref[...]).astype(BF16)
    ckv_ref[...] = ckv
    kr_ref[...] = pltpu.roll(kr_rot, 64, axis=1)[:, 0:32]
    for r, v in zip((hq_ref, hlf_ref, hk_ref, hv_ref, hg_ref), hg5):
        r[...] = v
    glu_ref[...] = glu


def _inproj_sample_kernel(h_ref, win_ref, qn_ref, wuq_ref, kvn_ref, cq_ref, sq_ref, ck_ref, sk_ref, lb_ref,
                          wukt_ref,
                          qatt_ref, qlat_ref, ckv_ref, kr_ref, hq_ref, hlf_ref, hk_ref, hv_ref, hg_ref,
                          glu_ref, *, ql, kv, heads, dn):
    x = h_ref[...].astype(BF16)
    qg, ckv, kr_rot, hg5, glu = _proj_common(x, win_ref, qn_ref, wuq_ref, kvn_ref, cq_ref, sq_ref, ck_ref, sk_ref,
                                             lb_ref, ql=ql, kv=kv, heads=heads)
    for g in range(heads):
        qatt_ref[:, g * LANES:(g + 1) * LANES] = qg[g].astype(BF16)
        qn = qg[g][:, 0:dn].astype(BF16)
        qlat_ref[:, g * kv:(g + 1) * kv] = _dot(qn, wukt_ref[g]).astype(BF16)
    ckv_ref[...] = ckv
    kr_ref[...] = pltpu.roll(kr_rot, 64, axis=1)[:, 0:32]
    for r, v in zip((hq_ref, hlf_ref, hk_ref, hv_ref, hg_ref), hg5):
        r[...] = v
    glu_ref[...] = glu


def _inproj(h_all, lw, tabs, *, tile0, ntiles, tab_period, sample):
    d = h_all.shape[1]
    t = ntiles * TM
    ql, kv, heads = lw["ql"], lw["kv"], lw["heads"]
    tab_spec = pl.BlockSpec((TM, LANES), lambda i: (i % tab_period, 0))
    in_specs = [pl.BlockSpec((TM, d), lambda i: (tile0 + i, 0)),
                _full(lw["w_in"].shape), _full((1, ql)), _full(lw["w_uq"].shape), _full((1, kv)),
                tab_spec, tab_spec, tab_spec, tab_spec, _full((8, 256))]
    args = [h_all, lw["w_in"], lw["q_norm"], lw["w_uq"], lw["kv_norm"], *tabs, lw["lbc"]]

    def tok(n, dt):
        return jax.ShapeDtypeStruct((t, n), dt), pl.BlockSpec((TM, n), lambda i: (i, 0))

    tail = [tok(kv, F32), tok(32, F32)] + [tok(256, F32)] * 6
    if sample:
        in_specs += [_full(lw["w_ukt"].shape)]
        args += [lw["w_ukt"]]
        outs = [tok(heads * LANES, BF16), tok(heads * kv, BF16)] + tail
        body = functools.partial(_inproj_sample_kernel, ql=ql, kv=kv, heads=heads, dn=lw["dn"])
        name = "inproj_sample"
    else:
        in_specs += [_full(lw["w_uk"].shape), _full(lw["w_uv"].shape)]
        args += [lw["w_uk"], lw["w_uv"]]
        outs = [tok(heads * LANES, BF16), tok(heads * LANES, BF16), tok(lw["w_uv"].shape[1], BF16)] + tail
        body = functools.partial(_inproj_prompt_kernel, ql=ql, kv=kv, heads=heads)
        name = "inproj_prompt"
    return pl.pallas_call(
        body, grid=(ntiles,), in_specs=in_specs,
        out_specs=[o[1] for o in outs], out_shape=[o[0] for o in outs],
        compiler_params=_cp("arbitrary"), name=name)(*args)


def _attn_prompt_kernel(q_ref, k_ref, v_ref, o_ref, *, seq, tq, dv):
    lane = lax.broadcasted_iota(I32, (1, LANES), 1)
    row = lax.broadcasted_iota(I32, (tq, tq), 0)
    col = lax.broadcasted_iota(I32, (tq, tq), 1)
    causal = col <= row
    vall = v_ref[...]
    vh = [jnp.where((lane >= hh * dv) & (lane < (hh + 1) * dv), vall, jnp.zeros_like(vall)) for hh in range(2)]
    for qi in range(seq // tq):
        q0, q1 = qi * tq, (qi + 1) * tq
        out = None
        for hh in range(2):
            hs = slice(hh * LANES, (hh + 1) * LANES)
            q = q_ref[q0:q1, hs]
            sd = lax.dot_general(q, k_ref[q0:q1, hs], _NT, preferred_element_type=F32)
            sd = jnp.where(causal, sd, -jnp.inf)
            m = jnp.max(sd, axis=1, keepdims=True)
            if qi > 0:
                so = lax.dot_general(q, k_ref[0:q0, hs], _NT, preferred_element_type=F32)
                m = jnp.maximum(m, jnp.max(so, axis=1, keepdims=True))
                po = jnp.exp(so - m)
            pd = jnp.exp(sd - m)
            l = jnp.sum(pd, axis=1, keepdims=True)
            o = _dot(pd.astype(BF16), vh[hh][q0:q1, :])
            if qi > 0:
                l = l + jnp.sum(po, axis=1, keepdims=True)
                o = o + _dot(po.astype(BF16), vh[hh][0:q0, :])
            o = o / l
            out = o if out is None else out + o
        o_ref[q0:q1, :] = out.astype(o_ref.dtype)


def _attn_prompt(qatt, katt, vatt, *, batch, seq, heads, dv):
    assert heads % 2 == 0 and 2 * dv == LANES
    tq = min(512, seq)
    q3 = qatt.reshape(batch, seq, heads * LANES)
    k3 = katt.reshape(batch, seq, heads * LANES)
    v3 = vatt.reshape(batch, seq, heads * dv)
    out = pl.pallas_call(
        functools.partial(_attn_prompt_kernel, seq=seq, tq=tq, dv=dv),
        grid=(batch, heads // 2),
        in_specs=[pl.BlockSpec((None, seq, 2 * LANES), lambda b, h: (b, 0, h)),
                  pl.BlockSpec((None, seq, 2 * LANES), lambda b, h: (b, 0, h)),
                  pl.BlockSpec((None, seq, LANES), lambda b, h: (b, 0, h))],
        out_specs=pl.BlockSpec((None, seq, LANES), lambda b, h: (b, 0, h)),
        out_shape=jax.ShapeDtypeStruct((batch, seq, heads * dv), BF16),
        compiler_params=_cp("arbitrary", "arbitrary"), name="attn_prompt")(q3, k3, v3)
    return out.reshape(batch * seq, heads * dv)


def _attn_sample_kernel(pt_ref, qlat_ref, qatt_ref, cnew_ref, knew_ref, *rest, pages, nj, heads, tnew, dn, dr):
    del pt_ref
    ckv_refs, kr_refs = rest[:pages], rest[pages:2 * pages]
    o_ref, m_sc, l_sc, acc_sc = rest[2 * pages:]
    j = pl.program_id(1)

    @pl.when(j == 0)
    def _():
        m_sc[...] = jnp.full_like(m_sc, -jnp.inf)
        l_sc[...] = jnp.zeros_like(l_sc)
        acc_sc[...] = jnp.zeros_like(acc_sc)

    ql = qlat_ref[...]
    qr = qatt_ref[:, dn:dn + dr]
    kcs, ss = [], []
    for i in range(pages):
        kc = ckv_refs[i][...].astype(BF16)
        kr = kr_refs[i][...].astype(BF16)
        kcs.append(kc)
        ss.append(lax.dot_general(ql, kc, _NT, preferred_element_type=F32)
                  + lax.dot_general(qr, kr, _NT, preferred_element_type=F32))
    s = jnp.concatenate(ss, axis=1)
    m_old = m_sc[...]
    m_new = jnp.maximum(m_old, jnp.max(s, axis=1, keepdims=True))
    alpha = jnp.exp(m_old - m_new)
    p = jnp.exp(s - m_new)
    l_sc[...] = alpha * l_sc[...] + jnp.sum(p, axis=1, keepdims=True)
    pb = p.astype(BF16)
    pv = _dot(pb[:, 0:LANES], kcs[0])
    for i in range(1, pages):
        pv = pv + _dot(pb[:, i * LANES:(i + 1) * LANES], kcs[i])
    acc_sc[...] = alpha * acc_sc[...] + pv
    m_sc[...] = m_new

    @pl.when(j == nj - 1)
    def _():
        qlf, qrf = ql.astype(F32), qr.astype(F32)
        cn, kn = cnew_ref[...], knew_ref[...]
        trow = lax.shift_right_logical(lax.broadcasted_iota(I32, (heads * tnew, 1), 0), int(math.log2(heads)))
        sn = []
        for t in range(tnew):
            st = (jnp.sum(qlf * cn[t:t + 1, :], axis=1, keepdims=True)
                  + jnp.sum(qrf * kn[t:t + 1, :], axis=1, keepdims=True))
            sn.append(jnp.where(trow >= t, st, -jnp.inf))
        m1 = m_sc[...]
        m2 = m1
        for t in range(tnew):
            m2 = jnp.maximum(m2, sn[t])
        a2 = jnp.exp(m1 - m2)
        l2 = a2 * l_sc[...]
        acc = a2 * acc_sc[...]
        for t in range(tnew):
            pt = jnp.exp(sn[t] - m2)
            l2 = l2 + pt
            acc = acc + pt * cn[t:t + 1, :]
        o_ref[...] = acc / l2


def _attn_sample(page_table, qlat, qatt, ckv_new, kr_new, cache_ckv, cache_krope, layer, *, heads, tnew, dn):
    nseq, npages = page_table.shape
    psize, kv = cache_ckv.shape[2], cache_ckv.shape[3]
    dr = cache_krope.shape[3]
    pages = math.gcd(npages, 8)
    nj = npages // pages
    rows = heads * tnew
    q3 = qlat.reshape(nseq, rows, kv)
    qa3 = qatt.reshape(nseq, rows, LANES)
    c3 = ckv_new.reshape(nseq, tnew, kv)
    k3 = kr_new.reshape(nseq, tnew, dr)

    def page_spec(width, i):
        return pl.BlockSpec((None, None, psize, width), lambda b, j, pt: (layer, pt[b, j * pages + i], 0, 0))

    in_specs = ([pl.BlockSpec((None, rows, kv), lambda b, j, pt: (b, 0, 0)),
                 pl.BlockSpec((None, rows, LANES), lambda b, j, pt: (b, 0, 0)),
                 pl.BlockSpec((None, tnew, kv), lambda b, j, pt: (b, 0, 0)),
                 pl.BlockSpec((None, tnew, dr), lambda b, j, pt: (b, 0, 0))]
                + [page_spec(kv, i) for i in range(pages)] + [page_spec(dr, i) for i in range(pages)])
    out = pl.pallas_call(
        functools.partial(_attn_sample_kernel, pages=pages, nj=nj, heads=heads, tnew=tnew, dn=dn, dr=dr),
        grid_spec=pltpu.PrefetchScalarGridSpec(
            num_scalar_prefetch=1, grid=(nseq, nj), in_specs=in_specs,
            out_specs=pl.BlockSpec((None, rows, kv), lambda b, j, pt: (b, 0, 0)),
            scratch_shapes=[pltpu.VMEM((rows, 1), F32), pltpu.VMEM((rows, 1), F32), pltpu.VMEM((rows, kv), F32)]),
        out_shape=jax.ShapeDtypeStruct((nseq, rows, kv), F32),
        compiler_params=_cp("arbitrary", "arbitrary"), name="attn_sample")(
            page_table, q3, qa3, c3, k3, *([cache_ckv] * pages), *([cache_krope] * pages))
    return out.reshape(nseq * tnew, heads * kv)


def _hgrn_kernel(q_ref, lf_ref, k_ref, v_ref, g_ref, s0_ref, ng_ref, o_ref, sfin_ref, st_sc, *, bt, c, nsteps, dh):
    j = pl.program_id(1)
    w = q_ref.shape[-1]
    sh = int(math.log2(dh))
    r = lax.broadcasted_iota(I32, (w, w), 0)
    cc = lax.broadcasted_iota(I32, (w, w), 1)
    bones = jnp.where(lax.shift_right_logical(r, sh) == lax.shift_right_logical(cc, sh), 1.0, 0.0).astype(F32)
    bones_b = bones.astype(BF16)

    @pl.when(j == 0)
    def _():
        e = jnp.where(lax.broadcasted_iota(I32, (dh, w), 0) == (lax.broadcasted_iota(I32, (dh, w), 1) & (dh - 1)),
                      1.0, 0.0).astype(F32)

        def init(b, carry):
            sbd = _dot_hi(s0_ref[b], e) * bones
            st_sc[b] = sbd.T
            return carry
        lax.fori_loop(0, bt, init, 0)

    lincl = jnp.where(lax.broadcasted_iota(I32, (c, c), 0) >= lax.broadcasted_iota(I32, (c, c), 1), 1.0, 0.0).astype(F32)
    srow = lax.broadcasted_iota(I32, (c, 1), 0)
    ng = ng_ref[...]

    def step(b, carry):
        q, lf, k, v, g = q_ref[b], lf_ref[b], k_ref[b], v_ref[b], g_ref[b]
        a = _dot_hi(lincl, lf)
        a_last = a[c - 1:c, :]
        ws = []
        for t in range(c):
            d = jnp.where(srow <= t, a[t:t + 1, :] - a, -jnp.inf)
            ws.append(jnp.exp(d) * (q[t:t + 1, :] * k))
        p = _dot(jnp.concatenate(ws, axis=0).astype(BF16), bones_b)
        o = jnp.concatenate([jnp.sum(p[t * c:(t + 1) * c, :] * v, axis=0, keepdims=True) for t in range(c)], axis=0)
        st = st_sc[b]
        o = o + lax.dot_general((q * jnp.exp(a)).astype(BF16), st.astype(BF16), _NT, preferred_element_type=F32)
        kd = (k * jnp.exp(a_last - a)).astype(BF16)
        st_sc[b] = st * jnp.exp(a_last) + lax.dot_general(v.astype(BF16), kd, _TN, preferred_element_type=F32) * bones
        ms = _dot_hi(o * o, bones) * (1.0 / dh)
        o_ref[b] = o * lax.rsqrt(ms + EPS) * ng * (g * jax.nn.sigmoid(g))
        return carry
    lax.fori_loop(0, bt, step, 0)

    @pl.when(j == nsteps - 1)
    def _():
        et = jnp.where((lax.broadcasted_iota(I32, (w, dh), 0) & (dh - 1)) == lax.broadcasted_iota(I32, (w, dh), 1),
                       1.0, 0.0).astype(F32)

        def fin(b, carry):
            sfin_ref[b] = _dot_hi(st_sc[b].T, et)
            return carry
        lax.fori_loop(0, bt, fin, 0)


def _hgrn(q, lf, k, v, g, s0, norm_g, *, c, bt, dh):
    batch, t, w = q.shape
    nsteps = t // c
    tok = pl.BlockSpec((bt, c, w), lambda bi, j: (bi, j, 0))
    st = pl.BlockSpec((bt, w, dh), lambda bi, j: (bi, 0, 0))
    return pl.pallas_call(
        functools.partial(_hgrn_kernel, bt=bt, c=c, nsteps=nsteps, dh=dh),
        grid=(batch // bt, nsteps),
        in_specs=[tok, tok, tok, tok, tok, st, _full((1, w))],
        out_specs=[tok, st],
        out_shape=[jax.ShapeDtypeStruct((batch, t, w), F32), jax.ShapeDtypeStruct((batch, w, dh), F32)],
        scratch_shapes=[pltpu.VMEM((bt, w, w), F32)],
        compiler_params=_cp("arbitrary", "arbitrary"), name="hgrn")(q, lf, k, v, g, s0, norm_g)


def _conv_kernel(glu_ref, buf_ref, w_ref, cb_ref, lg_ref, lb_ref, o_ref, nbuf_ref, ext_sc, *, tt, taps, nt):
    t = pl.program_id(1)
    hist = CONV_HIST

    @pl.when(t == 0)
    def _():
        ext_sc[0:hist, :] = buf_ref[...]

    ext_sc[hist:hist + tt, :] = glu_ref[...]
    base = hist - (taps - 1)
    acc = w_ref[0:1, :] * ext_sc[base:base + tt, :]
    for j in range(1, taps):
        acc = acc + w_ref[j:j + 1, :] * ext_sc[base + j:base + j + tt, :]
    y = _ln(acc + cb_ref[...], lg_ref[...], lb_ref[...])
    o_ref[...] = y * jax.nn.sigmoid(y)

    @pl.when(t == nt - 1)
    def _():
        nbuf_ref[...] = ext_sc[tt + base:tt + hist, :]

    ext_sc[0:hist, :] = ext_sc[tt:tt + hist, :]


def _conv(glu, buf_padded, w, cb, lg, lb, *, tt):
    batch, t, ch = glu.shape
    taps = w.shape[0]
    nt = t // tt
    return pl.pallas_call(
        functools.partial(_conv_kernel, tt=tt, taps=taps, nt=nt),
        grid=(batch, nt),
        in_specs=[pl.BlockSpec((None, tt, ch), lambda b, i: (b, i, 0)),
                  pl.BlockSpec((None, CONV_HIST, ch), lambda b, i: (b, 0, 0)),
                  _full(w.shape), _full((1, ch)), _full((1, ch)), _full((1, ch))],
        out_specs=[pl.BlockSpec((None, tt, ch), lambda b, i: (b, i, 0)),
                   pl.BlockSpec((None, taps - 1, ch), lambda b, i: (b, 0, 0))],
        out_shape=[jax.ShapeDtypeStruct((batch, t, ch), F32), jax.ShapeDtypeStruct((batch, taps - 1, ch), F32)],
        scratch_shapes=[pltpu.VMEM((CONV_HIST + tt, ch), F32)],
        compiler_params=_cp("arbitrary", "arbitrary"), name="conv")(glu, buf_padded, w, cb, lg, lb)


def _outproj_kernel(h_ref, oa_ref, ob_ref, oc_ref, olat_ref, obs_ref, ocs_ref, wuv_ref, wout_ref, g_ref, b_ref,
                    wr_ref, br_ref, h1_ref, route_ref, routet_ref, mix_sc, *, ntp, alpha, heads, kv, dv, ne, ng):
    i = pl.program_id(0)
    da = heads * dv
    wb = ob_ref.shape[1]

    @pl.when(i < ntp)
    def _():
        mix_sc[...] = (_dot(oa_ref[...], wout_ref[0:da, :])
                       + _dot(ob_ref[...].astype(BF16), wout_ref[da:da + wb, :])
                       + _dot(oc_ref[...].astype(BF16), wout_ref[da + wb:, :]))

    @pl.when(i >= ntp)
    def _():
        mix = (_dot(obs_ref[...].astype(BF16), wout_ref[da:da + wb, :])
               + _dot(ocs_ref[...].astype(BF16), wout_ref[da + wb:, :]))
        for h in range(heads):
            oh = _dot(olat_ref[:, h * kv:(h + 1) * kv].astype(BF16), wuv_ref[h])
            mix = mix + _dot(oh.astype(BF16), wout_ref[h * dv:(h + 1) * dv, :])
        mix_sc[...] = mix

    h1 = _ln(alpha * h_ref[...] + mix_sc[...], g_ref[...], b_ref[...])
    h1_ref[...] = h1
    logits = _dot_hi(h1, wr_ref[...]) + br_ref[...]
    tm = logits.shape[0]
    lane = lax.broadcasted_iota(I32, (tm, LANES), 1).astype(F32)
    big = jnp.float32(1e9)
    epg = ne // ng
    is_g = (lane >= ne) & (lane < ne + ng)
    gl = jnp.where(is_g, logits, -jnp.inf)
    gmax = jnp.max(gl, axis=1, keepdims=True)
    gidx = jnp.min(jnp.where(gl == gmax, lane, big), axis=1, keepdims=True) - ne
    p_g = 1.0 / jnp.sum(jnp.where(is_g, jnp.exp(logits - gmax), 0.0), axis=1, keepdims=True)
    lo = gidx * epg
    el = jnp.where((lane >= lo) & (lane < lo + epg), logits, -jnp.inf)
    v1 = jnp.max(el, axis=1, keepdims=True)
    e1 = jnp.min(jnp.where(el == v1, lane, big), axis=1, keepdims=True)
    el2 = jnp.where(lane == e1, -jnp.inf, el)
    v2 = jnp.max(el2, axis=1, keepdims=True)
    e2 = jnp.min(jnp.where(el2 == v2, lane, big), axis=1, keepdims=True)
    tt = jnp.exp(v2 - v1)
    w1 = p_g / (1.0 + tt)
    w2 = w1 * tt
    route = jnp.where(lane == 0, e1, jnp.where(lane == 1, e2, jnp.where(lane == 2, w1, jnp.where(lane == 3, w2, 0.0))))
    route_ref[...] = route
    routet_ref[...] = route.T[0:8, :]


def _outproj(h_all, oa_p, ob_p, oc_p, olat_s, ob_s, oc_s, lw, *, ntp, nts, alpha):
    t_all, d = h_all.shape
    heads, kv, dv = lw["heads"], lw["kv"], lw["dv"]
    nt = ntp + nts

    def pspec(n):
        return pl.BlockSpec((TM, n), lambda i: (jnp.minimum(i, ntp - 1), 0))

    def sspec(n):
        return pl.BlockSpec((TM, n), lambda i: (jnp.maximum(i - ntp, 0), 0))

    tok = pl.BlockSpec((TM, d), lambda i: (i, 0))
    return pl.pallas_call(
        functools.partial(_outproj_kernel, ntp=ntp, alpha=alpha, heads=heads, kv=kv, dv=dv, ne=lw["ne"], ng=lw["ng"]),
        grid=(nt,),
        in_specs=[tok, pspec(oa_p.shape[1]), pspec(ob_p.shape[1]), pspec(oc_p.shape[1]),
                  sspec(olat_s.shape[1]), sspec(ob_s.shape[1]), sspec(oc_s.shape[1]),
                  _full(lw["w_uv_h"].shape), _full(lw["w_out"].shape), _full((1, d)), _full((1, d)),
                  _full(lw["w_r"].shape), _full((1, LANES))],
        out_specs=[tok, pl.BlockSpec((TM, LANES), lambda i: (i, 0)), pl.BlockSpec((8, TM), lambda i: (0, i))],
        out_shape=[jax.ShapeDtypeStruct((t_all, d), F32), jax.ShapeDtypeStruct((t_all, LANES), F32),
                   jax.ShapeDtypeStruct((8, t_all), F32)],
        scratch_shapes=[pltpu.VMEM((TM, d), F32)],
        compiler_params=_cp("arbitrary"), name="outproj")(
            h_all, oa_p, ob_p, oc_p, olat_s, ob_s, oc_s, lw["w_uv_h"], lw["w_out"], lw["ln1_g"], lw["ln1_b"],
            lw["w_r"], lw["b_r"])


def _plan_kernel(rt_ref, dest_ref, be_ref, meta_ref, cnt_sc, seg_sc, carry_sc, *, bm, ne, nbp):
    p = pl.program_id(0)
    first = pl.program_id(1) == 0
    tm = rt_ref.shape[1]
    e1, e2 = rt_ref[0:1, :], rt_ref[1:2, :]
    sub = lax.broadcasted_iota(I32, (LANES, tm), 0).astype(F32)
    oh1 = sub == e1
    oh2 = sub == e2
    ohs = jnp.where(oh1, 1.0, 0.0) + jnp.where(oh2, 1.0, 0.0)
    tile_cnt = jnp.sum(ohs, axis=1, keepdims=True)

    @pl.when((p == 0) & first)
    def _():
        cnt_sc[...] = jnp.zeros_like(cnt_sc)

    @pl.when(p == 0)
    def _():
        cnt_sc[...] = cnt_sc[...] + tile_cnt

    @pl.when((p == 1) & first)
    def _():
        cnt = cnt_sc[...]
        padded = jnp.ceil(cnt * (1.0 / bm)) * bm
        r = lax.broadcasted_iota(I32, (LANES, LANES), 0)
        cidx = lax.broadcasted_iota(I32, (LANES, LANES), 1)
        lstrict = jnp.where(cidx < r, 1.0, 0.0).astype(F32)
        seg_start = _dot_hi(lstrict, jnp.broadcast_to(padded, (LANES, LANES)))[:, 0:1]
        seg_sc[...] = seg_start
        carry_sc[...] = jnp.zeros_like(carry_sc)
        seg_end = seg_start + padded
        jl = lax.broadcasted_iota(I32, (LANES, nbp), 1).astype(F32) * bm
        be = jnp.sum(jnp.where(seg_end <= jl, 1.0, 0.0), axis=0, keepdims=True)
        be_ref[...] = jnp.minimum(be, ne - 1).astype(I32)
        total = jnp.max(seg_end, axis=0, keepdims=True)
        meta_ref[...] = jnp.broadcast_to(total * (1.0 / bm), (1, LANES)).astype(I32)

    @pl.when(p == 1)
    def _():
        rr = lax.broadcasted_iota(I32, (tm, tm), 0)
        ccol = lax.broadcasted_iota(I32, (tm, tm), 1)
        ustrict = jnp.where(rr < ccol, 1.0, 0.0).astype(BF16)
        base = _dot(ohs.astype(BF16), ustrict) + carry_sc[...] + seg_sc[...]
        d1 = jnp.sum(jnp.where(oh1, base, 0.0), axis=0, keepdims=True)
        d2 = jnp.sum(jnp.where(oh2, base, 0.0), axis=0, keepdims=True)
        dest_ref[0:1, :] = d1.astype(I32)
        dest_ref[1:2, :] = d2.astype(I32)
        carry_sc[...] = carry_sc[...] + tile_cnt


def _plan(route_t, *, ne, nb_max):
    t_all = route_t.shape[1]
    tm = 512 if t_all % 512 == 0 else TM
    nt = t_all // tm
    nbp = pl.cdiv(nb_max, LANES) * LANES
    return pl.pallas_call(
        functools.partial(_plan_kernel, bm=MOE_BM, ne=ne, nbp=nbp),
        grid=(2, nt),
        in_specs=[pl.BlockSpec((8, tm), lambda p, i: (0, i))],
        out_specs=[pl.BlockSpec((2, tm), lambda p, i: (0, i * p)), _full((1, nbp)), _full((1, LANES))],
        out_shape=[jax.ShapeDtypeStruct((2, t_all), I32), jax.ShapeDtypeStruct((1, nbp), I32),
                   jax.ShapeDtypeStruct((1, LANES), I32)],
        scratch_shapes=[pltpu.VMEM((LANES, 1), F32)] * 3,
        compiler_params=_cp("arbitrary", "arbitrary"), name="moe_plan")(route_t)


def _scatter_kernel(d1_ref, d2_ref, h_hbm, rows_in, rows_hbm, sem, *, tm):
    del rows_in
    base = pl.program_id(0) * tm

    def start(r, carry):
        src = h_hbm.at[pl.ds(base + r, 1)]
        pltpu.make_async_copy(src, rows_hbm.at[pl.ds(d1_ref[0, 0, r], 1)], sem.at[0]).start()
        pltpu.make_async_copy(src, rows_hbm.at[pl.ds(d2_ref[0, 0, r], 1)], sem.at[1]).start()
        return carry
    lax.fori_loop(0, tm, start, 0)

    def wait(r, carry):
        pltpu.make_async_copy(h_hbm.at[pl.ds(0, 1)], rows_hbm.at[pl.ds(0, 1)], sem.at[0]).wait()
        pltpu.make_async_copy(h_hbm.at[pl.ds(0, 1)], rows_hbm.at[pl.ds(0, 1)], sem.at[1]).wait()
        return carry
    lax.fori_loop(0, tm, wait, 0)


def _scatter(h1, d1, d2, nrows):
    t_all, d = h1.shape
    tm = d1.shape[2]
    dspec = pl.BlockSpec((1, 1, tm), lambda i: (i, 0, 0), memory_space=pltpu.SMEM)
    return pl.pallas_call(
        functools.partial(_scatter_kernel, tm=tm),
        grid=(t_all // tm,),
        in_specs=[dspec, dspec, pl.BlockSpec(memory_space=pl.ANY), pl.BlockSpec(memory_space=pl.ANY)],
        out_specs=pl.BlockSpec(memory_space=pl.ANY),
        out_shape=jax.ShapeDtypeStruct((nrows, d), F32),
        scratch_shapes=[pltpu.SemaphoreType.DMA((2,))],
        input_output_aliases={3: 0},
        compiler_params=_cp("arbitrary"), name="moe_scatter")(d1, d2, h1, jnp.zeros((nrows, d), F32))


def _ffn_kernel(be_ref, na_ref, x_ref, wg_ref, wu_ref, wd_ref, y_ref, wgb, wub, wdb):
    i = pl.program_id(0)

    @pl.when(i < na_ref[0])
    def _():
        @pl.when((i == 0) | (be_ref[i] != be_ref[jnp.maximum(i - 1, 0)]))
        def _():
            wgb[...] = wg_ref[...].astype(BF16)
            wub[...] = wu_ref[...].astype(BF16)
            wdb[...] = wd_ref[...].astype(BF16)

        x = x_ref[...].astype(BF16)
        g = _dot(x, wgb[...])
        u = _dot(x, wub[...])
        hdn = (g * jax.nn.sigmoid(g) * u).astype(BF16)
        y_ref[...] = _dot(hdn, wdb[...])

    @pl.when(i >= na_ref[0])
    def _():
        y_ref[...] = jnp.zeros_like(y_ref)


def _ffn(rows, be, nact, w_gate, w_up, w_down, layer):
    nrows, d = rows.shape
    ff = w_gate.shape[3]
    nb = nrows // MOE_BM

    def blk(i, be_r, na_r):
        return jnp.minimum(i, na_r[0] - 1)

    def wspec(a, b):
        return pl.BlockSpec((None, None, a, b), lambda i, be_r, na_r: (layer, be_r[blk(i, be_r, na_r)], 0, 0))

    row_spec = pl.BlockSpec((MOE_BM, d), lambda i, be_r, na_r: (blk(i, be_r, na_r), 0))
    return pl.pallas_call(
        _ffn_kernel,
        grid_spec=pltpu.PrefetchScalarGridSpec(
            num_scalar_prefetch=2, grid=(nb,),
            in_specs=[row_spec, wspec(d, ff), wspec(d, ff), wspec(ff, d)],
            out_specs=pl.BlockSpec((MOE_BM, d), lambda i, be_r, na_r: (i, 0)),
            scratch_shapes=[pltpu.VMEM((d, ff), BF16), pltpu.VMEM((d, ff), BF16), pltpu.VMEM((ff, d), BF16)]),
        out_shape=jax.ShapeDtypeStruct((nrows, d), F32),
        compiler_params=_cp("arbitrary"), name="moe_ffn")(be, nact, rows, w_gate, w_up, w_down)


def _combine_kernel(d1_ref, d2_ref, h1_ref, route_ref, yb_hbm, g_ref, b_ref, out_ref, buf1, buf2, sem, *, tm, alpha):
    def start(r, carry):
        pltpu.make_async_copy(yb_hbm.at[pl.ds(d1_ref[0, 0, r], 1)], buf1.at[pl.ds(r, 1)], sem.at[0]).start()
        pltpu.make_async_copy(yb_hbm.at[pl.ds(d2_ref[0, 0, r], 1)], buf2.at[pl.ds(r, 1)], sem.at[1]).start()
        return carry
    lax.fori_loop(0, tm, start, 0)

    def wait(r, carry):
        pltpu.make_async_copy(yb_hbm.at[pl.ds(0, 1)], buf1.at[pl.ds(0, 1)], sem.at[0]).wait()
        pltpu.make_async_copy(yb_hbm.at[pl.ds(0, 1)], buf2.at[pl.ds(0, 1)], sem.at[1]).wait()
        return carry
    lax.fori_loop(0, tm, wait, 0)

    moe = route_ref[:, 2:3] * buf1[...] + route_ref[:, 3:4] * buf2[...]
    out_ref[...] = _ln(alpha * h1_ref[...] + moe, g_ref[...], b_ref[...])


def _combine(h1, route, yb, d1, d2, g, b, *, alpha):
    t_all, d = h1.shape
    tm = d1.shape[2]
    dspec = pl.BlockSpec((1, 1, tm), lambda i: (i, 0, 0), memory_space=pltpu.SMEM)
    tok = pl.BlockSpec((tm, d), lambda i: (i, 0))
    return pl.pallas_call(
        functools.partial(_combine_kernel, tm=tm, alpha=alpha),
        grid=(t_all // tm,),
        in_specs=[dspec, dspec, tok, pl.BlockSpec((tm, LANES), lambda i: (i, 0)),
                  pl.BlockSpec(memory_space=pl.ANY), _full((1, d)), _full((1, d))],
        out_specs=tok,
        out_shape=jax.ShapeDtypeStruct((t_all, d), F32),
        scratch_shapes=[pltpu.VMEM((tm, d), F32), pltpu.VMEM((tm, d), F32), pltpu.SemaphoreType.DMA((2,))],
        compiler_params=_cp("arbitrary"), name="moe_combine")(d1, d2, h1, route, yb, g, b)


def _rope_tables(pos, dr, scale):
    half = dr // 2
    inv = ROPE_BASE ** (-jnp.arange(half, dtype=F32) / half)
    ang = pos.astype(F32)[:, None] * inv[None, :]
    cos, sin = jnp.cos(ang), jnp.sin(ang)
    n = pos.shape[0]
    z64, z32, o64 = jnp.zeros((n, 64), F32), jnp.zeros((n, 32), F32), jnp.ones((n, 64), F32)
    cq = jnp.concatenate([o64, cos, cos, z32], axis=1) * scale
    sq = jnp.concatenate([z64, -sin, sin, z32], axis=1) * scale
    ck = jnp.concatenate([z64, cos, cos, z32], axis=1)
    sk = jnp.concatenate([z64, -sin, sin, z32], axis=1)
    return cq, sq, ck, sk


def _prep_layer(l, p):
    w_in = p["w_in"][l]
    d = w_in.shape[0]
    ql, kv = p["mla_q_norm"].shape[1], p["mla_kv_norm"].shape[1]
    heads, dn = p["mla_w_uk"].shape[2], p["mla_w_uk"].shape[3]
    dv = p["mla_w_uv"].shape[3]
    dr = p["mla_w_uq"].shape[2] // heads - dn
    assert dn == 64 and dr == 32 and dv == 64 and ql % LANES == 0 and kv % LANES == 0
    hw = p["hgrn_lb_logits"].shape[1]
    ch = p["conv_b"].shape[1]
    assert hw == 256 and ch == 256
    o = ql + kv
    kr = w_in[:, o:o + dr]
    x1, x2 = kr[:, :dr // 2], kr[:, dr // 2:]
    seg = jnp.concatenate([jnp.zeros((d, 64), F32), x1, x2, x2, x1], axis=1)
    w_in_p = jnp.concatenate([w_in[:, :o], seg, w_in[:, o + dr:]], axis=1).astype(BF16)
    uq = p["mla_w_uq"][l].reshape(ql, heads, dn + dr)
    u1, u2 = uq[:, :, dn:dn + dr // 2], uq[:, :, dn + dr // 2:]
    w_uq = jnp.concatenate([uq[:, :, :dn], u1, u2, u2, u1], axis=2).reshape(ql, heads * LANES).astype(BF16)
    uk = p["mla_w_uk"][l]
    w_uk = jnp.concatenate([uk, jnp.zeros_like(uk)], axis=2).reshape(kv, heads * LANES).astype(BF16)
    w_ukt = jnp.transpose(uk, (1, 2, 0)).astype(BF16)
    uv = p["mla_w_uv"][l]
    lb = p["lb_all"][l]
    lbc = jnp.zeros((8, hw), F32).at[0].set(jnp.log(lb)).at[1].set(jnp.log1p(-lb)).at[2].set(1.0 - lb)
    ne, ng = p["router_e_w"].shape[2], p["router_g_w"].shape[2]
    w_r = jnp.concatenate([p["router_e_w"][l], p["router_g_w"][l], jnp.zeros((d, LANES - ne - ng), F32)], axis=1)
    b_r = jnp.concatenate([p["router_e_b"][l], p["router_g_b"][l], jnp.zeros((LANES - ne - ng,), F32)])[None, :]
    return dict(
        ql=ql, kv=kv, heads=heads, dn=dn, dv=dv, dr=dr, ne=ne, ng=ng,
        w_in=w_in_p, q_norm=p["mla_q_norm"][l][None, :], w_uq=w_uq, kv_norm=p["mla_kv_norm"][l][None, :],
        w_uk=w_uk, w_ukt=w_ukt, w_uv=uv.reshape(kv, heads * dv).astype(BF16),
        w_uv_h=jnp.transpose(uv, (1, 0, 2)).astype(BF16), lbc=lbc,
        hgrn_norm=jnp.tile(p["hgrn_norm"][l], hw // p["hgrn_norm"].shape[1])[None, :],
        conv_w=p["conv_w"][l], conv_b=p["conv_b"][l][None, :],
        conv_ln_g=p["conv_ln_g"][l][None, :], conv_ln_b=p["conv_ln_b"][l][None, :],
        w_out=p["w_out"][l].astype(BF16), ln1_g=p["ln1_g"][l][None, :], ln1_b=p["ln1_b"][l][None, :],
        ln2_g=p["ln2_g"][l][None, :], ln2_b=p["ln2_b"][l][None, :], w_r=w_r, b_r=b_r)


def kernel(x_prompt, x_sample, cache_ckv, cache_krope, state_hgrn, state_conv, page_table, w_in, mla_q_norm, mla_w_uq, mla_kv_norm, mla_w_uk, mla_w_uv, hgrn_lb_logits, hgrn_norm, conv_w, conv_b, conv_ln_g, conv_ln_b, w_out, ln1_g, ln1_b, ln2_g, ln2_b, router_g_w, router_g_b, router_e_w, router_e_b, exp_w_gate, exp_w_up, exp_w_down):
    batch, seq, d = x_prompt.shape
    nseq, tnew, _ = x_sample.shape
    depth = w_in.shape[0]
    t_p, t_s = batch * seq, nseq * tnew
    t_all = t_p + t_s
    assert t_p % TM == 0 and t_s % TM == 0 and seq % TM == 0 and seq % HGRN_C == 0
    ntp, nts = t_p // TM, t_s // TM
    past_len = page_table.shape[1] * cache_ckv.shape[2]
    alpha = (2 * depth) ** 0.25
    hb, dkb = state_hgrn.shape[2], state_hgrn.shape[3]
    taps = conv_w.shape[1]
    assert taps - 1 <= CONV_HIST

    lbp = jax.nn.softmax(hgrn_lb_logits.astype(F32), axis=0)
    lbcs = jnp.cumsum(lbp, axis=0)
    params = dict(w_in=w_in, mla_q_norm=mla_q_norm, mla_w_uq=mla_w_uq, mla_kv_norm=mla_kv_norm, mla_w_uk=mla_w_uk,
                  mla_w_uv=mla_w_uv, hgrn_lb_logits=hgrn_lb_logits, lb_all=lbcs - lbcs[0:1], hgrn_norm=hgrn_norm,
                  conv_w=conv_w, conv_b=conv_b, conv_ln_g=conv_ln_g, conv_ln_b=conv_ln_b, w_out=w_out,
                  ln1_g=ln1_g, ln1_b=ln1_b, ln2_g=ln2_g, ln2_b=ln2_b, router_g_w=router_g_w, router_g_b=router_g_b,
                  router_e_w=router_e_w, router_e_b=router_e_b)
    dn = mla_w_uk.shape[3]
    dr = cache_krope.shape[3]
    scale = (dn + dr) ** -0.5
    tabs_p = _rope_tables(jnp.arange(seq), dr, scale)
    tabs_s = _rope_tables(jnp.tile(past_len + jnp.arange(tnew), nseq), dr, scale)

    ne = router_e_w.shape[2]
    n_assign = 2 * t_all
    nb_max = (n_assign + ne * (MOE_BM - 1) + MOE_BM - 1) // MOE_BM
    nrows = nb_max * MOE_BM
    tm_plan = 512 if t_all % 512 == 0 else TM

    h_all = jnp.concatenate([x_prompt.reshape(t_p, d), x_sample.reshape(t_s, d)], axis=0)
    outs = {k: [] for k in ("ckv_p", "kr_p", "sh_p", "sc_p", "ckv_s", "kr_s", "sh_s", "sc_s")}
    hw = hb * dkb
    tpad = 8
    bt_s = math.gcd(nseq, 16)
    for l in range(depth):
        lw = _prep_layer(l, params)
        heads, kv, dv = lw["heads"], lw["kv"], lw["dv"]
        (qatt, katt, vatt, ckv_p, kr_p, hq, hlf, hk, hv, hg, glu) = _inproj(
            h_all, lw, tabs_p, tile0=0, ntiles=ntp, tab_period=seq // TM, sample=False)
        oa_p = _attn_prompt(qatt, katt, vatt, batch=batch, seq=seq, heads=heads, dv=dv)
        r3 = lambda a: a.reshape(batch, seq, a.shape[1])
        ob_p, s_p = _hgrn(r3(hq), r3(hlf), r3(hk), r3(hv), r3(hg), jnp.zeros((batch, hw, dkb), F32), lw["hgrn_norm"],
                          c=HGRN_C, bt=batch, dh=dkb)
        oc_p, buf_p = _conv(r3(glu), jnp.zeros((batch, CONV_HIST, glu.shape[1]), F32), lw["conv_w"], lw["conv_b"],
                            lw["conv_ln_g"], lw["conv_ln_b"], tt=TM)
        (qatt_s, qlat_s, ckv_s, kr_s, hq, hlf, hk, hv, hg, glu_s) = _inproj(
            h_all, lw, tabs_s, tile0=ntp, ntiles=nts, tab_period=nts, sample=True)
        olat_s = _attn_sample(page_table, qlat_s, qatt_s, ckv_s, kr_s, cache_ckv, cache_krope, l,
                              heads=heads, tnew=tnew, dn=lw["dn"])
        r3s = lambda a: jnp.pad(a.reshape(nseq, tnew, a.shape[1]), ((0, 0), (0, tpad - tnew), (0, 0)))
        ob_s, s_s = _hgrn(r3s(hq), r3s(hlf), r3s(hk), r3s(hv), r3s(hg), state_hgrn[l].reshape(nseq, hw, dkb),
                          lw["hgrn_norm"], c=tpad, bt=bt_s, dh=dkb)
        ob_s = ob_s[:, :tnew].reshape(t_s, hw)
        bufpad = jnp.pad(state_conv[l], ((0, 0), (CONV_HIST - (taps - 1), 0), (0, 0)))
        oc_s, buf_s = _conv(glu_s.reshape(nseq, tnew, -1), bufpad, lw["conv_w"], lw["conv_b"],
                            lw["conv_ln_g"], lw["conv_ln_b"], tt=tnew)
        h1, route, route_t = _outproj(h_all, oa_p, ob_p.reshape(t_p, hw), oc_p.reshape(t_p, -1), olat_s, ob_s,
                                      oc_s.reshape(t_s, -1), lw, ntp=ntp, nts=nts, alpha=alpha)
        dest, be, meta = _plan(route_t, ne=ne, nb_max=nb_max)
        d1 = dest[0].reshape(t_all // tm_plan, 1, tm_plan)
        d2 = dest[1].reshape(t_all // tm_plan, 1, tm_plan)
        rows = _scatter(h1, d1, d2, nrows)
        yb = _ffn(rows, be.reshape(-1), meta[0, 0:1], exp_w_gate, exp_w_up, exp_w_down, l)
        dc1 = dest[0].reshape(t_all // TM, 1, TM)
        dc2 = dest[1].reshape(t_all // TM, 1, TM)
        h_all = _combine(h1, route, yb, dc1, dc2, lw["ln2_g"], lw["ln2_b"], alpha=alpha)

        outs["ckv_p"].append(ckv_p.reshape(batch, seq, kv))
        outs["kr_p"].append(kr_p.reshape(batch, seq, dr))
        outs["sh_p"].append(s_p.reshape(batch, hb, dkb, dkb))
        outs["sc_p"].append(buf_p)
        outs["ckv_s"].append(ckv_s.reshape(nseq, tnew, kv))
        outs["kr_s"].append(kr_s.reshape(nseq, tnew, dr))
        outs["sh_s"].append(s_s.reshape(nseq, hb, dkb, dkb))
        outs["sc_s"].append(buf_s)
    st = lambda k: jnp.stack(outs[k])
    return (h_all[:t_p].reshape(batch, seq, d), h_all[t_p:].reshape(nseq, tnew, d),
            st("ckv_p"), st("kr_p"), st("sh_p"), st("sc_p"), st("ckv_s"), st("kr_s"), st("sh_s"), st("sc_s"))
```

```python
import functools
import math

import jax
import jax.numpy as jnp
from jax import lax
from jax.experimental import pallas as pl
from jax.experimental.pallas import tpu as pltpu

F32 = jnp.float32
BF16 = jnp.bfloat16
I32 = jnp.int32

LANES = 128
ROPE_BASE = 10000.0
EPS = 1e-6
TM = 256
MOE_BM = 256
HGRN_C = 16
CONV_HIST = 32
SAMPLE_PAGES = 16
VMEM_LIMIT = 48 * 1024 * 1024
_HI = lax.Precision.HIGHEST
_NT = (((1,), (1,)), ((), ()))
_TN = (((0,), (0,)), ((), ()))


def _cp(*sem):
    return pltpu.CompilerParams(dimension_semantics=sem, vmem_limit_bytes=VMEM_LIMIT)


def _dot(a, b):
    return jnp.dot(a, b, preferred_element_type=F32)


def _dot_hi(a, b):
    return jnp.dot(a, b, preferred_element_type=F32, precision=_HI)


def _rms(x, g):
    return x * lax.rsqrt(jnp.mean(x * x, axis=-1, keepdims=True) + EPS) * g


def _ln(x, g, b):
    mu = jnp.mean(x, axis=-1, keepdims=True)
    xc = x - mu
    var = jnp.mean(xc * xc, axis=-1, keepdims=True)
    return xc * lax.rsqrt(var + EPS) * g + b


def _full(shape):
    n = len(shape)
    return pl.BlockSpec(shape, lambda *_: (0,) * n)


def _proj_common(x, win_ref, qn_ref, wuq_ref, kvn_ref, cq_ref, sq_ref, ck_ref, sk_ref, lb_ref, *, ql, kv, heads):
    o_kr = ql + kv
    o_h = o_kr + LANES
    o_c = o_h + 4 * 256
    y = _rms(_dot(x, win_ref[:, 0:ql]), qn_ref[...]).astype(BF16)
    q = _dot(y, wuq_ref[...])
    cqt, sqt = cq_ref[...], sq_ref[...]
    qg = []
    for g in range(heads):
        t = q[:, g * LANES:(g + 1) * LANES]
        qg.append(t * cqt + pltpu.roll(t, 96, axis=1) * sqt)
    ckv = _rms(_dot(x, win_ref[:, ql:o_kr]), kvn_ref[...])
    kr = _dot(x, win_ref[:, o_kr:o_h])
    kr_rot = kr * ck_ref[...] + pltpu.roll(kr, 96, axis=1) * sk_ref[...]
    uh = _dot(x, win_ref[:, o_h:o_c])
    hq, z, hv, hg = uh[:, 0:256], uh[:, 256:512], uh[:, 512:768], uh[:, 768:1024]
    log_lb, log1m_lb, one_m_lb = lb_ref[0:1, :], lb_ref[1:2, :], lb_ref[2:3, :]
    log_sig = jnp.minimum(z, 0.0) - jnp.log1p(jnp.exp(-jnp.abs(z)))
    b = log1m_lb + log_sig
    hlf = jnp.maximum(log_lb, b) + jnp.log1p(jnp.exp(-jnp.abs(log_lb - b)))
    hk = one_m_lb * jax.nn.sigmoid(-z)
    uc = _dot(x, win_ref[:, o_c:o_c + 512])
    glu = uc[:, 0:256] * jax.nn.sigmoid(uc[:, 256:512])
    return qg, ckv, kr_rot, (hq, hlf, hk, hv, hg), glu


def _inproj_prompt_kernel(h_ref, win_ref, qn_ref, wuq_ref, kvn_ref, cq_ref, sq_ref, ck_ref, sk_ref, lb_ref,
                          wuk_ref, wuv_ref,
                          qatt_ref, katt_ref, vatt_ref, ckv_ref, kr_ref, hq_ref, hlf_ref, hk_ref, hv_ref, hg_ref,
                          glu_ref, *, ql, kv, heads):
    x = h_ref[...].astype(BF16)
    qg, ckv, kr_rot, hg5, glu = _proj_common(x, win_ref, qn_ref, wuq_ref, kvn_ref, cq_ref, sq_ref, ck_ref, sk_ref,
                                             lb_ref, ql=ql, kv=kv, heads=heads)
    cb = ckv.astype(BF16)
    kn = _dot(cb, wuk_ref[...])
    for g in range(heads):
        sl = slice(g * LANES, (g + 1) * LANES)
        qatt_ref[:, sl] = qg[g].astype(BF16)
        katt_ref[:, sl] = (kn[:, sl] + kr_rot).astype(BF16)
    vatt_ref[...] = _dot(cb, wuv_ref[...]).astype(BF16)
    ckv_ref[...] = ckv
    kr_ref[...] = pltpu.roll(kr_rot, 64, axis=1)[:, 0:32]
    for r, v in zip((hq_ref, hlf_ref, hk_ref, hv_ref, hg_ref), hg5):
        r[...] = v
    glu_ref[...] = glu


def _inproj_sample_kernel(h_ref, win_ref, qn_ref, wuq_ref, kvn_ref, cq_ref, sq_ref, ck_ref, sk_ref, lb_ref,
                          wukt_ref,
                          qatt_ref, qlat_ref, ckv_ref, kr_ref, hq_ref, hlf_ref, hk_ref, hv_ref, hg_ref,
                          glu_ref, *, ql, kv, heads, dn):
    x = h_ref[...].astype(BF16)
    qg, ckv, kr_rot, hg5, glu = _proj_common(x, win_ref, qn_ref, wuq_ref, kvn_ref, cq_ref, sq_ref, ck_ref, sk_ref,
                                             lb_ref, ql=ql, kv=kv, heads=heads)
    for g in range(heads):
        qatt_ref[:, g * LANES:(g + 1) * LANES] = qg[g].astype(BF16)
        qn = qg[g][:, 0:dn].astype(BF16)
        qlat_ref[:, g * kv:(g + 1) * kv] = _dot(qn, wukt_ref[g]).astype(BF16)
    ckv_ref[...] = ckv
    kr_ref[...] = pltpu.roll(kr_rot, 64, axis=1)[:, 0:32]
    for r, v in zip((hq_ref, hlf_ref, hk_ref, hv_ref, hg_ref), hg5):
        r[...] = v
    glu_ref[...] = glu


def _inproj(h, lw, tabs, *, tab_period, sample):
    t, d = h.shape
    ntiles = t // TM
    ql, kv, heads = lw["ql"], lw["kv"], lw["heads"]
    tab_spec = pl.BlockSpec((TM, LANES), lambda i: (i % tab_period, 0))
    in_specs = [pl.BlockSpec((TM, d), lambda i: (i, 0)),
                _full(lw["w_in"].shape), _full((1, ql)), _full(lw["w_uq"].shape), _full((1, kv)),
                tab_spec, tab_spec, tab_spec, tab_spec, _full((8, 256))]
    args = [h, lw["w_in"], lw["q_norm"], lw["w_uq"], lw["kv_norm"], *tabs, lw["lbc"]]

    def tok(n, dt):
        return jax.ShapeDtypeStruct((t, n), dt), pl.BlockSpec((TM, n), lambda i: (i, 0))

    tail = [tok(kv, F32), tok(32, F32)] + [tok(256, F32)] * 6
    if sample:
        in_specs += [_full(lw["w_ukt"].shape)]
        args += [lw["w_ukt"]]
        outs = [tok(heads * LANES, BF16), tok(heads * kv, BF16)] + tail
        body = functools.partial(_inproj_sample_kernel, ql=ql, kv=kv, heads=heads, dn=lw["dn"])
        name = "inproj_sample"
    else:
        in_specs += [_full(lw["w_uk"].shape), _full(lw["w_uv"].shape)]
        args += [lw["w_uk"], lw["w_uv"]]
        outs = [tok(heads * LANES, BF16), tok(heads * LANES, BF16), tok(lw["w_uv"].shape[1], BF16)] + tail
        body = functools.partial(_inproj_prompt_kernel, ql=ql, kv=kv, heads=heads)
        name = "inproj_prompt"
    return pl.pallas_call(
        body, grid=(ntiles,), in_specs=in_specs,
        out_specs=[o[1] for o in outs], out_shape=[o[0] for o in outs],
        compiler_params=_cp("arbitrary"), name=name)(*args)


def _attn_prompt_kernel(q_ref, k_ref, v_ref, o_ref, *, seq, tq, dv):
    lane = lax.broadcasted_iota(I32, (1, LANES), 1)
    row = lax.broadcasted_iota(I32, (tq, tq), 0)
    col = lax.broadcasted_iota(I32, (tq, tq), 1)
    causal = col <= row
    vall = v_ref[...]
    vh = [jnp.where((lane >= hh * dv) & (lane < (hh + 1) * dv), vall, jnp.zeros_like(vall)) for hh in range(2)]
    for qi in range(seq // tq):
        q0, q1 = qi * tq, (qi + 1) * tq
        out = None
        for hh in range(2):
            hs = slice(hh * LANES, (hh + 1) * LANES)
            q = q_ref[q0:q1, hs]
            sd = lax.dot_general(q, k_ref[q0:q1, hs], _NT, preferred_element_type=F32)
            sd = jnp.where(causal, sd, -jnp.inf)
            m = jnp.max(sd, axis=1, keepdims=True)
            if qi > 0:
                so = lax.dot_general(q, k_ref[0:q0, hs], _NT, preferred_element_type=F32)
                m = jnp.maximum(m, jnp.max(so, axis=1, keepdims=True))
                po = jnp.exp(so - m)
            pd = jnp.exp(sd - m)
            l = jnp.sum(pd, axis=1, keepdims=True)
            o = _dot(pd.astype(BF16), vh[hh][q0:q1, :])
            if qi > 0:
                l = l + jnp.sum(po, axis=1, keepdims=True)
                o = o + _dot(po.astype(BF16), vh[hh][0:q0, :])
            o = o / l
            out = o if out is None else out + o
        o_ref[q0:q1, :] = out.astype(o_ref.dtype)


def _attn_prompt(qatt, katt, vatt, *, batch, seq, heads, dv):
    assert heads % 2 == 0 and 2 * dv == LANES
    tq = min(512, seq)
    q3 = qatt.reshape(batch, seq, heads * LANES)
    k3 = katt.reshape(batch, seq, heads * LANES)
    v3 = vatt.reshape(batch, seq, heads * dv)
    out = pl.pallas_call(
        functools.partial(_attn_prompt_kernel, seq=seq, tq=tq, dv=dv),
        grid=(batch, heads // 2),
        in_specs=[pl.BlockSpec((None, seq, 2 * LANES), lambda b, h: (b, 0, h)),
                  pl.BlockSpec((None, seq, 2 * LANES), lambda b, h: (b, 0, h)),
                  pl.BlockSpec((None, seq, LANES), lambda b, h: (b, 0, h))],
        out_specs=pl.BlockSpec((None, seq, LANES), lambda b, h: (b, 0, h)),
        out_shape=jax.ShapeDtypeStruct((batch, seq, heads * dv), BF16),
        compiler_params=_cp("arbitrary", "arbitrary"), name="attn_prompt")(q3, k3, v3)
    return out.reshape(batch * seq, heads * dv)


def _attn_sample_kernel(pt_ref, qlat_ref, qatt_ref, cnew_ref, knew_ref, ckv_hbm, krt_hbm, o_ref, kbuf, rbuf, sem,
                        *, layer, pages, nchunks, nseq, heads, tnew, dn, dr):
    b = pl.program_id(0)

    def fetch(seq, chunk, slot):
        for i in range(pages):
            page = pt_ref[seq, chunk * pages + i]
            pltpu.make_async_copy(ckv_hbm.at[layer, page], kbuf.at[slot, i], sem.at[0, slot]).start()
            pltpu.make_async_copy(krt_hbm.at[layer, page], rbuf.at[slot, i], sem.at[1, slot]).start()

    @pl.when(b == 0)
    def _():
        fetch(0, 0, 0)

    ql = qlat_ref[...]
    qr = qatt_ref[:, dn:dn + dr]
    rows, kv = ql.shape

    def chunk_step(c, carry):
        m_old, l_old, acc = carry
        g = b * nchunks + c
        slot = lax.rem(g, 2)

        @pl.when(c + 1 < nchunks)
        def _():
            fetch(b, c + 1, 1 - slot)

        @pl.when((c + 1 == nchunks) & (b + 1 < nseq))
        def _():
            fetch(b + 1, 0, 1 - slot)

        pltpu.make_async_copy(ckv_hbm.at[layer, pl.ds(0, pages)], kbuf.at[slot], sem.at[0, slot]).wait()
        pltpu.make_async_copy(krt_hbm.at[layer, pl.ds(0, pages)], rbuf.at[slot], sem.at[1, slot]).wait()
        kcs, ss = [], []
        for i in range(pages):
            kc = kbuf[slot, i].astype(BF16)
            krt = rbuf[slot, i].astype(BF16)
            kcs.append(kc)
            ss.append(lax.dot_general(ql, kc, _NT, preferred_element_type=F32) + _dot(qr, krt))
        s = jnp.concatenate(ss, axis=1)
        m_new = jnp.maximum(m_old, jnp.max(s, axis=1, keepdims=True))
        alpha = jnp.exp(m_old - m_new)
        p = jnp.exp(s - m_new)
        l_new = alpha * l_old + jnp.sum(p, axis=1, keepdims=True)
        pb = p.astype(BF16)
        pv = _dot(pb[:, 0:LANES], kcs[0])
        for i in range(1, pages):
            pv = pv + _dot(pb[:, i * LANES:(i + 1) * LANES], kcs[i])
        return m_new, l_new, alpha * acc + pv

    m1, l1, acc1 = lax.fori_loop(
        0, nchunks, chunk_step,
        (jnp.full((rows, 1), -jnp.inf, F32), jnp.zeros((rows, 1), F32), jnp.zeros((rows, kv), F32)))

    qlf, qrf = ql.astype(F32), qr.astype(F32)
    cn, kn = cnew_ref[...], knew_ref[...]
    trow = lax.shift_right_logical(lax.broadcasted_iota(I32, (rows, 1), 0), int(math.log2(heads)))
    sn = []
    for t in range(tnew):
        st = (jnp.sum(qlf * cn[t:t + 1, :], axis=1, keepdims=True)
              + jnp.sum(qrf * kn[t:t + 1, :], axis=1, keepdims=True))
        sn.append(jnp.where(trow >= t, st, -jnp.inf))
    m2 = m1
    for t in range(tnew):
        m2 = jnp.maximum(m2, sn[t])
    a2 = jnp.exp(m1 - m2)
    l2 = a2 * l1
    acc = a2 * acc1
    for t in range(tnew):
        pt = jnp.exp(sn[t] - m2)
        l2 = l2 + pt
        acc = acc + pt * cn[t:t + 1, :]
    o_ref[...] = acc / l2


def _attn_sample(page_table, qlat, qatt, ckv_new, kr_new, cache_ckv, cache_krope, layer, *, heads, tnew, dn):
    nseq, npages = page_table.shape
    psize, kv = cache_ckv.shape[2], cache_ckv.shape[3]
    dr = cache_krope.shape[3]
    pages = math.gcd(npages, SAMPLE_PAGES)
    nchunks = npages // pages
    krope_t = jnp.swapaxes(cache_krope, 2, 3)
    rows = heads * tnew
    q3 = qlat.reshape(nseq, rows, kv)
    qa3 = qatt.reshape(nseq, rows, LANES)
    c3 = ckv_new.reshape(nseq, tnew, kv)
    k3 = kr_new.reshape(nseq, tnew, dr)
    in_specs = [pl.BlockSpec((None, rows, kv), lambda b, pt: (b, 0, 0)),
                pl.BlockSpec((None, rows, LANES), lambda b, pt: (b, 0, 0)),
                pl.BlockSpec((None, tnew, kv), lambda b, pt: (b, 0, 0)),
                pl.BlockSpec((None, tnew, dr), lambda b, pt: (b, 0, 0)),
                pl.BlockSpec(memory_space=pl.ANY), pl.BlockSpec(memory_space=pl.ANY)]
    out = pl.pallas_call(
        functools.partial(_attn_sample_kernel, layer=layer, pages=pages, nchunks=nchunks, nseq=nseq, heads=heads,
                          tnew=tnew, dn=dn, dr=dr),
        grid_spec=pltpu.PrefetchScalarGridSpec(
            num_scalar_prefetch=1, grid=(nseq,), in_specs=in_specs,
            out_specs=pl.BlockSpec((None, rows, kv), lambda b, pt: (b, 0, 0)),
            scratch_shapes=[pltpu.VMEM((2, pages, psize, kv), F32), pltpu.VMEM((2, pages, dr, psize), F32),
                            pltpu.SemaphoreType.DMA((2, 2))]),
        out_shape=jax.ShapeDtypeStruct((nseq, rows, kv), F32),
        compiler_params=_cp("arbitrary"), name="attn_sample")(page_table, q3, qa3, c3, k3, cache_ckv, krope_t)
    return out.reshape(nseq * tnew, heads * kv)


def _head_blocks(w, dh, dtype):
    sh = int(math.log2(dh))
    r = lax.shift_right_logical(lax.broadcasted_iota(I32, (w, w), 0), sh)
    c = lax.shift_right_logical(lax.broadcasted_iota(I32, (w, w), 1), sh)
    return jnp.where(r == c, 1.0, 0.0).astype(dtype)


def _hgrn_intra_kernel(q_ref, lf_ref, k_ref, v_ref, oi_ref, qd_ref, kd_ref, ea_ref, a_sc, *, c, dh):
    tm, w = q_ref.shape
    sh = int(math.log2(c))
    r = lax.broadcasted_iota(I32, (tm, tm), 0)
    cc = lax.broadcasted_iota(I32, (tm, tm), 1)
    same = lax.shift_right_logical(r, sh) == lax.shift_right_logical(cc, sh)
    tril = jnp.where(same & (cc <= r), 1.0, 0.0).astype(BF16)
    blk = jnp.where(same, 1.0, 0.0).astype(BF16)
    lf = lf_ref[...]
    hi = lf.astype(BF16)
    r1 = lf - hi.astype(F32)
    mid = r1.astype(BF16)
    lo = (r1 - mid.astype(F32)).astype(BF16)
    a = _dot(tril, hi) + _dot(tril, mid) + _dot(tril, lo)
    atot = _dot(blk, hi) + _dot(blk, mid) + _dot(blk, lo)
    a_sc[...] = a
    qd_ref[...] = q_ref[...] * jnp.exp(a)
    kd_ref[...] = k_ref[...] * jnp.exp(atot - a)
    ea_ref[...] = jnp.exp(atot)
    bones_b = _head_blocks(w, dh, BF16)
    srow = lax.broadcasted_iota(I32, (c, 1), 0)

    def chunk(ci, carry):
        sl = pl.ds(pl.multiple_of(ci * c, c), c)
        ac, qc, kc, vc = a_sc[sl, :], q_ref[sl, :], k_ref[sl, :], v_ref[sl, :]
        ws = []
        for t in range(c):
            d = jnp.where(srow <= t, ac[t:t + 1, :] - ac, -jnp.inf)
            ws.append(jnp.exp(d) * (qc[t:t + 1, :] * kc))
        p = _dot(jnp.concatenate(ws, axis=0).astype(BF16), bones_b)
        oi_ref[sl, :] = jnp.concatenate(
            [jnp.sum(p[t * c:(t + 1) * c, :] * vc, axis=0, keepdims=True) for t in range(c)], axis=0)
        return carry
    lax.fori_loop(0, tm // c, chunk, 0, unroll=2)


def _hgrn_seq_kernel(oi_ref, qd_ref, kd_ref, v_ref, g_ref, ea_ref, s0_ref, ng_ref, o_ref, sfin_ref, st_sc,
                     *, bt, nsteps, dh, unroll):
    j = pl.program_id(1)
    w = qd_ref.shape[-1]
    heads = w // dh
    bones = _head_blocks(w, dh, F32)
    bones_b = bones.astype(BF16)

    @pl.when(j == 0)
    def _():
        def init(b, carry):
            s0 = s0_ref[b]
            st_sc[b] = (jnp.concatenate([s0] * heads, axis=1) * bones).T
            return carry
        lax.fori_loop(0, bt, init, 0)

    ng = ng_ref[...]

    def step(b, carry):
        st = st_sc[b]
        g = g_ref[b]
        o = oi_ref[b] + lax.dot_general(qd_ref[b].astype(BF16), st.astype(BF16), _NT, preferred_element_type=F32)
        upd = lax.dot_general(v_ref[b].astype(BF16), kd_ref[b].astype(BF16), _TN, preferred_element_type=F32)
        st_sc[b] = st * ea_ref[b][0:1, :] + upd * bones
        ms = _dot((o * o).astype(BF16), bones_b) * (1.0 / dh)
        o_ref[b] = o * lax.rsqrt(ms + EPS) * ng * (g * jax.nn.sigmoid(g))
        return carry
    lax.fori_loop(0, bt, step, 0, unroll=unroll)

    @pl.when(j == nsteps - 1)
    def _():
        def fin(b, carry):
            sbd = st_sc[b].T
            acc = sbd[:, 0:dh]
            for h in range(1, heads):
                acc = acc + sbd[:, h * dh:(h + 1) * dh]
            sfin_ref[b] = acc
            return carry
        lax.fori_loop(0, bt, fin, 0)


def _hgrn(q, lf, k, v, g, s0, norm_g, *, c, bt, dh):
    batch, t, w = q.shape
    rows = batch * t
    tm = math.gcd(rows, TM)
    flat = lambda a: a.reshape(rows, w)
    tile = pl.BlockSpec((tm, w), lambda i: (i, 0))
    oi, qd, kd, ea = pl.pallas_call(
        functools.partial(_hgrn_intra_kernel, c=c, dh=dh),
        grid=(rows // tm,),
        in_specs=[tile] * 4, out_specs=[tile] * 4,
        out_shape=[jax.ShapeDtypeStruct((rows, w), F32)] * 4,
        scratch_shapes=[pltpu.VMEM((tm, w), F32)],
        compiler_params=_cp("arbitrary"), name="hgrn_intra")(flat(q), flat(lf), flat(k), flat(v))
    nsteps = t // c
    tok = pl.BlockSpec((bt, c, w), lambda bi, j: (bi, j, 0))
    st = pl.BlockSpec((bt, w, dh), lambda bi, j: (bi, 0, 0))
    b3 = lambda a: a.reshape(batch, t, w)
    return pl.pallas_call(
        functools.partial(_hgrn_seq_kernel, bt=bt, nsteps=nsteps, dh=dh, unroll=4),
        grid=(batch // bt, nsteps),
        in_specs=[tok] * 6 + [st, _full((1, w))],
        out_specs=[tok, st],
        out_shape=[jax.ShapeDtypeStruct((batch, t, w), F32), jax.ShapeDtypeStruct((batch, w, dh), F32)],
        scratch_shapes=[pltpu.VMEM((bt, w, w), F32)],
        compiler_params=_cp("arbitrary", "arbitrary"), name="hgrn_seq")(
            b3(oi), b3(qd), b3(kd), v, g, b3(ea), s0, norm_g)


def _conv_kernel(glu_ref, buf_ref, w_ref, cb_ref, lg_ref, lb_ref, o_ref, nbuf_ref, ext_sc, *, tt, taps, nt):
    t = pl.program_id(1)
    hist = CONV_HIST

    @pl.when(t == 0)
    def _():
        ext_sc[0:hist, :] = buf_ref[...]

    ext_sc[hist:hist + tt, :] = glu_ref[...]
    base = hist - (taps - 1)
    acc = w_ref[0:1, :] * ext_sc[base:base + tt, :]
    for j in range(1, taps):
        acc = acc + w_ref[j:j + 1, :] * ext_sc[base + j:base + j + tt, :]
    y = _ln(acc + cb_ref[...], lg_ref[...], lb_ref[...])
    o_ref[...] = y * jax.nn.sigmoid(y)

    @pl.when(t == nt - 1)
    def _():
        nbuf_ref[...] = ext_sc[tt + base:tt + hist, :]

    ext_sc[0:hist, :] = ext_sc[tt:tt + hist, :]


def _conv(glu, buf_padded, w, cb, lg, lb, *, tt):
    batch, t, ch = glu.shape
    taps = w.shape[0]
    nt = t // tt
    return pl.pallas_call(
        functools.partial(_conv_kernel, tt=tt, taps=taps, nt=nt),
        grid=(batch, nt),
        in_specs=[pl.BlockSpec((None, tt, ch), lambda b, i: (b, i, 0)),
                  pl.BlockSpec((None, CONV_HIST, ch), lambda b, i: (b, 0, 0)),
                  _full(w.shape), _full((1, ch)), _full((1, ch)), _full((1, ch))],
        out_specs=[pl.BlockSpec((None, tt, ch), lambda b, i: (b, i, 0)),
                   pl.BlockSpec((None, taps - 1, ch), lambda b, i: (b, 0, 0))],
        out_shape=[jax.ShapeDtypeStruct((batch, t, ch), F32), jax.ShapeDtypeStruct((batch, taps - 1, ch), F32)],
        scratch_shapes=[pltpu.VMEM((CONV_HIST + tt, ch), F32)],
        compiler_params=_cp("arbitrary", "arbitrary"), name="conv")(glu, buf_padded, w, cb, lg, lb)


def _outproj_kernel(hp_ref, hs_ref, oa_ref, ob_ref, oc_ref, olat_ref, obs_ref, ocs_ref, wuv_ref, wout_ref, g_ref, b_ref,
                    wr_ref, br_ref, h1_ref, route_ref, routet_ref, mix_sc, *, ntp, alpha, heads, kv, dv, ne, ng):
    i = pl.program_id(0)
    da = heads * dv
    wb = ob_ref.shape[1]

    @pl.when(i < ntp)
    def _():
        mix_sc[...] = (alpha * hp_ref[...] + _dot(oa_ref[...], wout_ref[0:da, :])
                       + _dot(ob_ref[...].astype(BF16), wout_ref[da:da + wb, :])
                       + _dot(oc_ref[...].astype(BF16), wout_ref[da + wb:, :]))

    @pl.when(i >= ntp)
    def _():
        mix = (alpha * hs_ref[...] + _dot(obs_ref[...].astype(BF16), wout_ref[da:da + wb, :])
               + _dot(ocs_ref[...].astype(BF16), wout_ref[da + wb:, :]))
        for h in range(heads):
            oh = _dot(olat_ref[:, h * kv:(h + 1) * kv].astype(BF16), wuv_ref[h])
            mix = mix + _dot(oh.astype(BF16), wout_ref[h * dv:(h + 1) * dv, :])
        mix_sc[...] = mix

    h1 = _ln(mix_sc[...], g_ref[...], b_ref[...])
    h1_ref[...] = h1
    logits = _dot_hi(h1, wr_ref[...]) + br_ref[...]
    tm = logits.shape[0]
    lane = lax.broadcasted_iota(I32, (tm, LANES), 1).astype(F32)
    big = jnp.float32(1e9)
    epg = ne // ng
    is_g = (lane >= ne) & (lane < ne + ng)
    gl = jnp.where(is_g, logits, -jnp.inf)
    gmax = jnp.max(gl, axis=1, keepdims=True)
    gidx = jnp.min(jnp.where(gl == gmax, lane, big), axis=1, keepdims=True) - ne
    p_g = 1.0 / jnp.sum(jnp.where(is_g, jnp.exp(logits - gmax), 0.0), axis=1, keepdims=True)
    lo = gidx * epg
    el = jnp.where((lane >= lo) & (lane < lo + epg), logits, -jnp.inf)
    v1 = jnp.max(el, axis=1, keepdims=True)
    e1 = jnp.min(jnp.where(el == v1, lane, big), axis=1, keepdims=True)
    el2 = jnp.where(lane == e1, -jnp.inf, el)
    v2 = jnp.max(el2, axis=1, keepdims=True)
    e2 = jnp.min(jnp.where(el2 == v2, lane, big), axis=1, keepdims=True)
    tt = jnp.exp(v2 - v1)
    w1 = p_g / (1.0 + tt)
    w2 = w1 * tt
    route = jnp.where(lane == 0, e1, jnp.where(lane == 1, e2, jnp.where(lane == 2, w1, jnp.where(lane == 3, w2, 0.0))))
    route_ref[...] = route
    routet_ref[...] = route.T[0:8, :]


def _outproj(h_p, h_s, oa_p, ob_p, oc_p, olat_s, ob_s, oc_s, lw, *, ntp, nts, alpha):
    d = h_p.shape[1]
    heads, kv, dv = lw["heads"], lw["kv"], lw["dv"]
    nt = ntp + nts
    t_all = nt * TM

    def pspec(n):
        return pl.BlockSpec((TM, n), lambda i: (jnp.minimum(i, ntp - 1), 0))

    def sspec(n):
        return pl.BlockSpec((TM, n), lambda i: (jnp.maximum(i - ntp, 0), 0))

    tok = pl.BlockSpec((TM, d), lambda i: (i, 0))
    return pl.pallas_call(
        functools.partial(_outproj_kernel, ntp=ntp, alpha=alpha, heads=heads, kv=kv, dv=dv, ne=lw["ne"], ng=lw["ng"]),
        grid=(nt,),
        in_specs=[pspec(d), sspec(d), pspec(oa_p.shape[1]), pspec(ob_p.shape[1]), pspec(oc_p.shape[1]),
                  sspec(olat_s.shape[1]), sspec(ob_s.shape[1]), sspec(oc_s.shape[1]),
                  _full(lw["w_uv_h"].shape), _full(lw["w_out"].shape), _full((1, d)), _full((1, d)),
                  _full(lw["w_r"].shape), _full((1, LANES))],
        out_specs=[tok, pl.BlockSpec((TM, LANES), lambda i: (i, 0)), pl.BlockSpec((8, TM), lambda i: (0, i))],
        out_shape=[jax.ShapeDtypeStruct((t_all, d), F32), jax.ShapeDtypeStruct((t_all, LANES), F32),
                   jax.ShapeDtypeStruct((8, t_all), F32)],
        scratch_shapes=[pltpu.VMEM((TM, d), F32)],
        compiler_params=_cp("arbitrary"), name="outproj")(
            h_p, h_s, oa_p, ob_p, oc_p, olat_s, ob_s, oc_s, lw["w_uv_h"], lw["w_out"], lw["ln1_g"], lw["ln1_b"],
            lw["w_r"], lw["b_r"])


def _plan_kernel(rt_ref, dest_ref, be_ref, meta_ref, seg_ref, cnt_sc, seg_sc, carry_sc, *, bm, ne, nbp):
    p = pl.program_id(0)
    first = pl.program_id(1) == 0
    tm = rt_ref.shape[1]
    e1, e2 = rt_ref[0:1, :], rt_ref[1:2, :]
    sub = lax.broadcasted_iota(I32, (LANES, tm), 0).astype(F32)
    oh1 = sub == e1
    oh2 = sub == e2
    ohs = jnp.where(oh1, 1.0, 0.0) + jnp.where(oh2, 1.0, 0.0)
    tile_cnt = jnp.sum(ohs, axis=1, keepdims=True)

    @pl.when((p == 0) & first)
    def _():
        cnt_sc[...] = jnp.zeros_like(cnt_sc)

    @pl.when(p == 0)
    def _():
        cnt_sc[...] = cnt_sc[...] + tile_cnt

    @pl.when((p == 1) & first)
    def _():
        cnt = cnt_sc[...]
        padded = jnp.ceil(cnt * (1.0 / bm)) * bm
        r = lax.broadcasted_iota(I32, (LANES, LANES), 0)
        cidx = lax.broadcasted_iota(I32, (LANES, LANES), 1)
        lstrict = jnp.where(cidx < r, 1.0, 0.0).astype(F32)
        seg_start = _dot_hi(lstrict, jnp.broadcast_to(padded, (LANES, LANES)))[:, 0:1]
        seg_sc[...] = seg_start
        carry_sc[...] = jnp.zeros_like(carry_sc)
        seg_end = seg_start + padded
        jl = lax.broadcasted_iota(I32, (LANES, nbp), 1).astype(F32) * bm
        be = jnp.sum(jnp.where(seg_end <= jl, 1.0, 0.0), axis=0, keepdims=True)
        be_ref[...] = jnp.minimum(be, ne - 1).astype(I32)
        total = jnp.max(seg_end, axis=0, keepdims=True)
        meta_ref[...] = jnp.broadcast_to(total * (1.0 / bm), (1, LANES)).astype(I32)
        eye = r == cidx
        seg_ref[0:1, :] = jnp.sum(jnp.where(eye, seg_start, 0.0), axis=0, keepdims=True).astype(I32)
        seg_ref[1:2, :] = jnp.sum(jnp.where(eye, seg_end, 0.0), axis=0, keepdims=True).astype(I32)

    @pl.when(p == 1)
    def _():
        rr = lax.broadcasted_iota(I32, (tm, tm), 0)
        ccol = lax.broadcasted_iota(I32, (tm, tm), 1)
        ustrict = jnp.where(rr < ccol, 1.0, 0.0).astype(BF16)
        base = _dot(ohs.astype(BF16), ustrict) + carry_sc[...] + seg_sc[...]
        d1 = jnp.sum(jnp.where(oh1, base, 0.0), axis=0, keepdims=True)
        d2 = jnp.sum(jnp.where(oh2, base, 0.0), axis=0, keepdims=True)
        dest_ref[0:1, :] = d1.astype(I32)
        dest_ref[1:2, :] = d2.astype(I32)
        carry_sc[...] = carry_sc[...] + tile_cnt


def _plan(route_t, *, ne, nb_max):
    t_all = route_t.shape[1]
    tm = 512 if t_all % 512 == 0 else TM
    nt = t_all // tm
    nbp = pl.cdiv(nb_max, LANES) * LANES
    return pl.pallas_call(
        functools.partial(_plan_kernel, bm=MOE_BM, ne=ne, nbp=nbp),
        grid=(2, nt),
        in_specs=[pl.BlockSpec((8, tm), lambda p, i: (0, i))],
        out_specs=[pl.BlockSpec((2, tm), lambda p, i: (0, i * p)), _full((1, nbp)), _full((1, LANES)),
                   _full((2, LANES))],
        out_shape=[jax.ShapeDtypeStruct((2, t_all), I32), jax.ShapeDtypeStruct((1, nbp), I32),
                   jax.ShapeDtypeStruct((1, LANES), I32), jax.ShapeDtypeStruct((2, LANES), I32)],
        scratch_shapes=[pltpu.VMEM((LANES, 1), F32)] * 3,
        compiler_params=_cp("arbitrary", "arbitrary"), name="moe_plan")(route_t)


def _scatter_kernel(d1_ref, d2_ref, seg_ref, meta_ref, h_ref, rows_hbm, zbuf, sem, zsem, *, tm, ne, bm, nb):
    @pl.when(pl.program_id(0) == 0)
    def _():
        zbuf[...] = jnp.zeros_like(zbuf)
        for e in range(ne):
            @pl.when(seg_ref[1, e] > seg_ref[0, e])
            def _():
                last = pl.multiple_of(seg_ref[1, e] - bm, bm)
                pltpu.make_async_copy(zbuf, rows_hbm.at[pl.ds(last, bm)], zsem).start()

        def ztail(jb, carry):
            pltpu.make_async_copy(zbuf, rows_hbm.at[pl.ds(pl.multiple_of(jb * bm, bm), bm)], zsem).start()
            return carry
        lax.fori_loop(meta_ref[0, 0], nb, ztail, 0)
        for e in range(ne):
            @pl.when(seg_ref[1, e] > seg_ref[0, e])
            def _():
                pltpu.make_async_copy(zbuf, rows_hbm.at[pl.ds(0, bm)], zsem).wait()

        def zwait(jb, carry):
            pltpu.make_async_copy(zbuf, rows_hbm.at[pl.ds(0, bm)], zsem).wait()
            return carry
        lax.fori_loop(meta_ref[0, 0], nb, zwait, 0)

    def start(r, carry):
        src = h_ref.at[pl.ds(r, 1)]
        pltpu.make_async_copy(src, rows_hbm.at[pl.ds(d1_ref[0, 0, r], 1)], sem.at[0]).start()
        pltpu.make_async_copy(src, rows_hbm.at[pl.ds(d2_ref[0, 0, r], 1)], sem.at[1]).start()
        return carry
    lax.fori_loop(0, tm, start, 0, unroll=8)
    pltpu.make_async_copy(h_ref, rows_hbm.at[pl.ds(0, tm)], sem.at[0]).wait()
    pltpu.make_async_copy(h_ref, rows_hbm.at[pl.ds(0, tm)], sem.at[1]).wait()


def _scatter(h1, d1, d2, seg, meta, nrows, *, ne):
    t_all, d = h1.shape
    tm = d1.shape[2]
    dspec = pl.BlockSpec((1, 1, tm), lambda i: (i, 0, 0), memory_space=pltpu.SMEM)
    smem = pl.BlockSpec(memory_space=pltpu.SMEM)
    return pl.pallas_call(
        functools.partial(_scatter_kernel, tm=tm, ne=ne, bm=MOE_BM, nb=nrows // MOE_BM),
        grid=(t_all // tm,),
        in_specs=[dspec, dspec, smem, smem, pl.BlockSpec((tm, d), lambda i: (i, 0))],
        out_specs=pl.BlockSpec(memory_space=pl.ANY),
        out_shape=jax.ShapeDtypeStruct((nrows, d), F32),
        scratch_shapes=[pltpu.VMEM((MOE_BM, d), F32), pltpu.SemaphoreType.DMA((2,)), pltpu.SemaphoreType.DMA(())],
        compiler_params=_cp("arbitrary"), name="moe_scatter")(d1, d2, seg, meta, h1)


def _ffn_kernel(be_ref, na_ref, x_ref, wg_ref, wu_ref, wd_ref, y_ref, wgb, wub, wdb):
    i = pl.program_id(0)

    @pl.when(i < na_ref[0])
    def _():
        @pl.when((i == 0) | (be_ref[i] != be_ref[jnp.maximum(i - 1, 0)]))
        def _():
            wgb[...] = wg_ref[...].astype(BF16)
            wub[...] = wu_ref[...].astype(BF16)
            wdb[...] = wd_ref[...].astype(BF16)

        x = x_ref[...].astype(BF16)
        g = _dot(x, wgb[...])
        u = _dot(x, wub[...])
        hdn = (g * jax.nn.sigmoid(g) * u).astype(BF16)
        y_ref[...] = _dot(hdn, wdb[...])

    @pl.when(i >= na_ref[0])
    def _():
        y_ref[...] = jnp.zeros_like(y_ref)


def _ffn(rows, be, nact, w_gate, w_up, w_down, layer):
    nrows, d = rows.shape
    ff = w_gate.shape[3]
    nb = nrows // MOE_BM

    def blk(i, be_r, na_r):
        return jnp.minimum(i, na_r[0] - 1)

    def wspec(a, b):
        return pl.BlockSpec((None, None, a, b), lambda i, be_r, na_r: (layer, be_r[blk(i, be_r, na_r)], 0, 0))

    row_spec = pl.BlockSpec((MOE_BM, d), lambda i, be_r, na_r: (blk(i, be_r, na_r), 0))
    return pl.pallas_call(
        _ffn_kernel,
        grid_spec=pltpu.PrefetchScalarGridSpec(
            num_scalar_prefetch=2, grid=(nb,),
            in_specs=[row_spec, wspec(d, ff), wspec(d, ff), wspec(ff, d)],
            out_specs=pl.BlockSpec((MOE_BM, d), lambda i, be_r, na_r: (i, 0)),
            scratch_shapes=[pltpu.VMEM((d, ff), BF16), pltpu.VMEM((d, ff), BF16), pltpu.VMEM((ff, d), BF16)]),
        out_shape=jax.ShapeDtypeStruct((nrows, d), F32),
        compiler_params=_cp("arbitrary"), name="moe_ffn")(be, nact, rows, w_gate, w_up, w_down)


def _combine_kernel(d1_ref, d2_ref, d1n_ref, d2n_ref, h1_ref, route_ref, yb_hbm, g_ref, b_ref, outp_ref, outs_ref,
                    buf1, buf2, sem, *, tm, nt, ntp, alpha):
    i = pl.program_id(0)
    slot = lax.rem(i, 2)

    def gather(da_ref, db_ref, s):
        def start(r, carry):
            pltpu.make_async_copy(yb_hbm.at[pl.ds(da_ref[0, 0, r], 1)], buf1.at[s, pl.ds(r, 1)], sem.at[0, s]).start()
            pltpu.make_async_copy(yb_hbm.at[pl.ds(db_ref[0, 0, r], 1)], buf2.at[s, pl.ds(r, 1)], sem.at[1, s]).start()
            return carry
        lax.fori_loop(0, tm, start, 0, unroll=8)

    @pl.when(i == 0)
    def _():
        gather(d1_ref, d2_ref, 0)

    @pl.when(i + 1 < nt)
    def _():
        gather(d1n_ref, d2n_ref, 1 - slot)

    pltpu.make_async_copy(yb_hbm.at[pl.ds(0, tm)], buf1.at[slot], sem.at[0, slot]).wait()
    pltpu.make_async_copy(yb_hbm.at[pl.ds(0, tm)], buf2.at[slot], sem.at[1, slot]).wait()
    moe = route_ref[:, 2:3] * buf1[slot] + route_ref[:, 3:4] * buf2[slot]
    out = _ln(alpha * h1_ref[...] + moe, g_ref[...], b_ref[...])

    @pl.when(i < ntp)
    def _():
        outp_ref[...] = out

    @pl.when(i >= ntp)
    def _():
        outs_ref[...] = out


def _combine(h1, route, yb, d1, d2, g, b, *, ntp, alpha):
    t_all, d = h1.shape
    tm = d1.shape[2]
    nt = t_all // tm
    dspec = pl.BlockSpec((1, 1, tm), lambda i: (i, 0, 0), memory_space=pltpu.SMEM)
    dnext = pl.BlockSpec((1, 1, tm), lambda i: (jnp.minimum(i + 1, nt - 1), 0, 0), memory_space=pltpu.SMEM)
    tok = pl.BlockSpec((tm, d), lambda i: (i, 0))
    return pl.pallas_call(
        functools.partial(_combine_kernel, tm=tm, nt=nt, ntp=ntp, alpha=alpha),
        grid=(nt,),
        in_specs=[dspec, dspec, dnext, dnext, tok, pl.BlockSpec((tm, LANES), lambda i: (i, 0)),
                  pl.BlockSpec(memory_space=pl.ANY), _full((1, d)), _full((1, d))],
        out_specs=[pl.BlockSpec((tm, d), lambda i: (jnp.minimum(i, ntp - 1), 0)),
                   pl.BlockSpec((tm, d), lambda i: (jnp.maximum(i - ntp, 0), 0))],
        out_shape=[jax.ShapeDtypeStruct((ntp * tm, d), F32), jax.ShapeDtypeStruct(((nt - ntp) * tm, d), F32)],
        scratch_shapes=[pltpu.VMEM((2, tm, d), F32), pltpu.VMEM((2, tm, d), F32), pltpu.SemaphoreType.DMA((2, 2))],
        compiler_params=_cp("arbitrary"), name="moe_combine")(d1, d2, d1, d2, h1, route, yb, g, b)


def _rope_tables(pos, dr, scale):
    half = dr // 2
    inv = ROPE_BASE ** (-jnp.arange(half, dtype=F32) / half)
    ang = pos.astype(F32)[:, None] * inv[None, :]
    cos, sin = jnp.cos(ang), jnp.sin(ang)
    n = pos.shape[0]
    z64, z32, o64 = jnp.zeros((n, 64), F32), jnp.zeros((n, 32), F32), jnp.ones((n, 64), F32)
    cq = jnp.concatenate([o64, cos, cos, z32], axis=1) * scale
    sq = jnp.concatenate([z64, -sin, sin, z32], axis=1) * scale
    ck = jnp.concatenate([z64, cos, cos, z32], axis=1)
    sk = jnp.concatenate([z64, -sin, sin, z32], axis=1)
    return cq, sq, ck, sk


def _prep_layer(l, p):
    w_in = p["w_in"][l]
    d = w_in.shape[0]
    ql, kv = p["mla_q_norm"].shape[1], p["mla_kv_norm"].shape[1]
    heads, dn = p["mla_w_uk"].shape[2], p["mla_w_uk"].shape[3]
    dv = p["mla_w_uv"].shape[3]
    dr = p["mla_w_uq"].shape[2] // heads - dn
    assert dn == 64 and dr == 32 and dv == 64 and ql % LANES == 0 and kv % LANES == 0
    hw = p["hgrn_lb_logits"].shape[1]
    ch = p["conv_b"].shape[1]
    assert hw == 256 and ch == 256
    o = ql + kv
    kr = w_in[:, o:o + dr]
    x1, x2 = kr[:, :dr // 2], kr[:, dr // 2:]
    seg = jnp.concatenate([jnp.zeros((d, 64), F32), x1, x2, x2, x1], axis=1)
    w_in_p = jnp.concatenate([w_in[:, :o], seg, w_in[:, o + dr:]], axis=1).astype(BF16)
    uq = p["mla_w_uq"][l].reshape(ql, heads, dn + dr)
    u1, u2 = uq[:, :, dn:dn + dr // 2], uq[:, :, dn + dr // 2:]
    w_uq = jnp.concatenate([uq[:, :, :dn], u1, u2, u2, u1], axis=2).reshape(ql, heads * LANES).astype(BF16)
    uk = p["mla_w_uk"][l]
    w_uk = jnp.concatenate([uk, jnp.zeros_like(uk)], axis=2).reshape(kv, heads * LANES).astype(BF16)
    w_ukt = jnp.transpose(uk, (1, 2, 0)).astype(BF16)
    uv = p["mla_w_uv"][l]
    lb = p["lb_all"][l]
    lbc = jnp.zeros((8, hw), F32).at[0].set(jnp.log(lb)).at[1].set(jnp.log1p(-lb)).at[2].set(1.0 - lb)
    ne, ng = p["router_e_w"].shape[2], p["router_g_w"].shape[2]
    w_r = jnp.concatenate([p["router_e_w"][l], p["router_g_w"][l], jnp.zeros((d, LANES - ne - ng), F32)], axis=1)
    b_r = jnp.concatenate([p["router_e_b"][l], p["router_g_b"][l], jnp.zeros((LANES - ne - ng,), F32)])[None, :]
    return dict(
        ql=ql, kv=kv, heads=heads, dn=dn, dv=dv, dr=dr, ne=ne, ng=ng,
        w_in=w_in_p, q_norm=p["mla_q_norm"][l][None, :], w_uq=w_uq, kv_norm=p["mla_kv_norm"][l][None, :],
        w_uk=w_uk, w_ukt=w_ukt, w_uv=uv.reshape(kv, heads * dv).astype(BF16),
        w_uv_h=jnp.transpose(uv, (1, 0, 2)).astype(BF16), lbc=lbc,
        hgrn_norm=jnp.tile(p["hgrn_norm"][l], hw // p["hgrn_norm"].shape[1])[None, :],
        conv_w=p["conv_w"][l], conv_b=p["conv_b"][l][None, :],
        conv_ln_g=p["conv_ln_g"][l][None, :], conv_ln_b=p["conv_ln_b"][l][None, :],
        w_out=p["w_out"][l].astype(BF16), ln1_g=p["ln1_g"][l][None, :], ln1_b=p["ln1_b"][l][None, :],
        ln2_g=p["ln2_g"][l][None, :], ln2_b=p["ln2_b"][l][None, :], w_r=w_r, b_r=b_r)


def kernel(x_prompt, x_sample, cache_ckv, cache_krope, state_hgrn, state_conv, page_table, w_in, mla_q_norm, mla_w_uq, mla_kv_norm, mla_w_uk, mla_w_uv, hgrn_lb_logits, hgrn_norm, conv_w, conv_b, conv_ln_g, conv_ln_b, w_out, ln1_g, ln1_b, ln2_g, ln2_b, router_g_w, router_g_b, router_e_w, router_e_b, exp_w_gate, exp_w_up, exp_w_down):
    batch, seq, d = x_prompt.shape
    nseq, tnew, _ = x_sample.shape
    depth = w_in.shape[0]
    t_p, t_s = batch * seq, nseq * tnew
    t_all = t_p + t_s
    assert t_p % TM == 0 and t_s % TM == 0 and seq % TM == 0 and seq % HGRN_C == 0
    ntp, nts = t_p // TM, t_s // TM
    past_len = page_table.shape[1] * cache_ckv.shape[2]
    alpha = (2 * depth) ** 0.25
    hb, dkb = state_hgrn.shape[2], state_hgrn.shape[3]
    taps = conv_w.shape[1]
    assert taps - 1 <= CONV_HIST

    lbp = jax.nn.softmax(hgrn_lb_logits.astype(F32), axis=0)
    lbcs = jnp.cumsum(lbp, axis=0)
    params = dict(w_in=w_in, mla_q_norm=mla_q_norm, mla_w_uq=mla_w_uq, mla_kv_norm=mla_kv_norm, mla_w_uk=mla_w_uk,
                  mla_w_uv=mla_w_uv, hgrn_lb_logits=hgrn_lb_logits, lb_all=lbcs - lbcs[0:1], hgrn_norm=hgrn_norm,
                  conv_w=conv_w, conv_b=conv_b, conv_ln_g=conv_ln_g, conv_ln_b=conv_ln_b, w_out=w_out,
                  ln1_g=ln1_g, ln1_b=ln1_b, ln2_g=ln2_g, ln2_b=ln2_b, router_g_w=router_g_w, router_g_b=router_g_b,
                  router_e_w=router_e_w, router_e_b=router_e_b)
    dn = mla_w_uk.shape[3]
    dr = cache_krope.shape[3]
    scale = (dn + dr) ** -0.5
    tabs_p = _rope_tables(jnp.arange(seq), dr, scale)
    tabs_s = _rope_tables(jnp.tile(past_len + jnp.arange(tnew), nseq), dr, scale)

    ne = router_e_w.shape[2]
    n_assign = 2 * t_all
    nb_max = (n_assign + ne * (MOE_BM - 1) + MOE_BM - 1) // MOE_BM
    nrows = nb_max * MOE_BM

    h_p, h_s = x_prompt.reshape(t_p, d), x_sample.reshape(t_s, d)
    outs = {k: [] for k in ("ckv_p", "kr_p", "sh_p", "sc_p", "ckv_s", "kr_s", "sh_s", "sc_s")}
    hw = hb * dkb
    tpad = 8
    bt_s = math.gcd(nseq, 16)
    for l in range(depth):
        lw = _prep_layer(l, params)
        heads, kv, dv = lw["heads"], lw["kv"], lw["dv"]
        (qatt, katt, vatt, ckv_p, kr_p, hq, hlf, hk, hv, hg, glu) = _inproj(
            h_p, lw, tabs_p, tab_period=seq // TM, sample=False)
        oa_p = _attn_prompt(qatt, katt, vatt, batch=batch, seq=seq, heads=heads, dv=dv)
        r3 = lambda a: a.reshape(batch, seq, a.shape[1])
        ob_p, s_p = _hgrn(r3(hq), r3(hlf), r3(hk), r3(hv), r3(hg), jnp.zeros((batch, hw, dkb), F32), lw["hgrn_norm"],
                          c=HGRN_C, bt=batch, dh=dkb)
        oc_p, buf_p = _conv(r3(glu), jnp.zeros((batch, CONV_HIST, glu.shape[1]), F32), lw["conv_w"], lw["conv_b"],
                            lw["conv_ln_g"], lw["conv_ln_b"], tt=TM)
        (qatt_s, qlat_s, ckv_s, kr_s, hq, hlf, hk, hv, hg, glu_s) = _inproj(
            h_s, lw, tabs_s, tab_period=nts, sample=True)
        olat_s = _attn_sample(page_table, qlat_s, qatt_s, ckv_s, kr_s, cache_ckv, cache_krope, l,
                              heads=heads, tnew=tnew, dn=lw["dn"])
        r3s = lambda a: jnp.pad(a.reshape(nseq, tnew, a.shape[1]), ((0, 0), (0, tpad - tnew), (0, 0)))
        ob_s, s_s = _hgrn(r3s(hq), r3s(hlf), r3s(hk), r3s(hv), r3s(hg), state_hgrn[l].reshape(nseq, hw, dkb),
                          lw["hgrn_norm"], c=tpad, bt=bt_s, dh=dkb)
        ob_s = ob_s[:, :tnew].reshape(t_s, hw)
        bufpad = jnp.pad(state_conv[l], ((0, 0), (CONV_HIST - (taps - 1), 0), (0, 0)))
        oc_s, buf_s = _conv(glu_s.reshape(nseq, tnew, -1), bufpad, lw["conv_w"], lw["conv_b"],
                            lw["conv_ln_g"], lw["conv_ln_b"], tt=tnew)
        h1, route, route_t = _outproj(h_p, h_s, oa_p, ob_p.reshape(t_p, hw), oc_p.reshape(t_p, -1), olat_s, ob_s,
                                      oc_s.reshape(t_s, -1), lw, ntp=ntp, nts=nts, alpha=alpha)
        dest, be, meta, seg = _plan(route_t, ne=ne, nb_max=nb_max)
        d1 = dest[0].reshape(t_all // TM, 1, TM)
        d2 = dest[1].reshape(t_all // TM, 1, TM)
        rows = _scatter(h1, d1, d2, seg, meta, nrows, ne=ne)
        yb = _ffn(rows, be.reshape(-1), meta[0, 0:1], exp_w_gate, exp_w_up, exp_w_down, l)
        h_p, h_s = _combine(h1, route, yb, d1, d2, lw["ln2_g"], lw["ln2_b"], ntp=ntp, alpha=alpha)

        outs["ckv_p"].append(ckv_p.reshape(batch, seq, kv))
        outs["kr_p"].append(kr_p.reshape(batch, seq, dr))
        outs["sh_p"].append(s_p.reshape(batch, hb, dkb, dkb))
        outs["sc_p"].append(buf_p)
        outs["ckv_s"].append(ckv_s.reshape(nseq, tnew, kv))
        outs["kr_s"].append(kr_s.reshape(nseq, tnew, dr))
        outs["sh_s"].append(s_s.reshape(nseq, hb, dkb, dkb))
        outs["sc_s"].append(buf_s)
    st = lambda k: jnp.stack(outs[k])
    return (h_p.reshape(batch, seq, d), h_s.reshape(nseq, tnew, d),
            st("ckv_p"), st("kr_p"), st("sh_p"), st("sc_p"), st("ckv_s"), st("kr_s"), st("sh_s"), st("sc_s"))
```

```python
import functools
import math

import jax
import jax.numpy as jnp
from jax import lax
from jax.experimental import pallas as pl
from jax.experimental.pallas import tpu as pltpu

F32 = jnp.float32
BF16 = jnp.bfloat16
I32 = jnp.int32

LANES = 128
ROPE_BASE = 10000.0
EPS = 1e-6
TM = 256
MOE_BM = 256
HGRN_C = 16
CONV_HIST = 32
SAMPLE_PAGES = 32
NSLOT = 3
VMEM_LIMIT = 48 * 1024 * 1024
_HI = lax.Precision.HIGHEST
_NT = (((1,), (1,)), ((), ()))
_TN = (((0,), (0,)), ((), ()))


def _cp(*sem):
    return pltpu.CompilerParams(dimension_semantics=sem, vmem_limit_bytes=VMEM_LIMIT)


def _dot(a, b):
    return jnp.dot(a, b, preferred_element_type=F32)


def _dot_hi(a, b):
    return jnp.dot(a, b, preferred_element_type=F32, precision=_HI)


def _rms(x, g):
    return x * lax.rsqrt(jnp.mean(x * x, axis=-1, keepdims=True) + EPS) * g


def _ln(x, g, b):
    mu = jnp.mean(x, axis=-1, keepdims=True)
    xc = x - mu
    var = jnp.mean(xc * xc, axis=-1, keepdims=True)
    return xc * lax.rsqrt(var + EPS) * g + b


def _full(shape):
    n = len(shape)
    return pl.BlockSpec(shape, lambda *_: (0,) * n)


def _proj_common(x, win_ref, qn_ref, wuq_ref, kvn_ref, cq_ref, sq_ref, ck_ref, sk_ref, lb_ref, *, ql, kv, heads):
    o_kr = ql + kv
    o_h = o_kr + LANES
    o_c = o_h + 4 * 256
    y = _rms(_dot(x, win_ref[:, 0:ql]), qn_ref[...]).astype(BF16)
    q = _dot(y, wuq_ref[...])
    cqt, sqt = cq_ref[...], sq_ref[...]
    qg = []
    for g in range(heads):
        t = q[:, g * LANES:(g + 1) * LANES]
        qg.append(t * cqt + pltpu.roll(t, 96, axis=1) * sqt)
    ckv = _rms(_dot(x, win_ref[:, ql:o_kr]), kvn_ref[...])
    kr = _dot(x, win_ref[:, o_kr:o_h])
    kr_rot = kr * ck_ref[...] + pltpu.roll(kr, 96, axis=1) * sk_ref[...]
    uh = _dot(x, win_ref[:, o_h:o_c])
    hq, z, hv, hg = uh[:, 0:256], uh[:, 256:512], uh[:, 512:768], uh[:, 768:1024]
    log_lb, log1m_lb, one_m_lb = lb_ref[0:1, :], lb_ref[1:2, :], lb_ref[2:3, :]
    log_sig = jnp.minimum(z, 0.0) - jnp.log1p(jnp.exp(-jnp.abs(z)))
    b = log1m_lb + log_sig
    hlf = jnp.maximum(log_lb, b) + jnp.log1p(jnp.exp(-jnp.abs(log_lb - b)))
    hk = one_m_lb * jax.nn.sigmoid(-z)
    uc = _dot(x, win_ref[:, o_c:o_c + 512])
    glu = uc[:, 0:256] * jax.nn.sigmoid(uc[:, 256:512])
    return qg, ckv, kr_rot, (hq, hlf, hk, hv, hg), glu


def _inproj_prompt_kernel(h_ref, win_ref, qn_ref, wuq_ref, kvn_ref, cq_ref, sq_ref, ck_ref, sk_ref, lb_ref,
                          wuk_ref, wuv_ref,
                          qatt_ref, katt_ref, vatt_ref, ckv_ref, kr_ref, hq_ref, hlf_ref, hk_ref, hv_ref, hg_ref,
                          glu_ref, *, ql, kv, heads):
    x = h_ref[...].astype(BF16)
    qg, ckv, kr_rot, hg5, glu = _proj_common(x, win_ref, qn_ref, wuq_ref, kvn_ref, cq_ref, sq_ref, ck_ref, sk_ref,
                                             lb_ref, ql=ql, kv=kv, heads=heads)
    cb = ckv.astype(BF16)
    kn = _dot(cb, wuk_ref[...])
    for g in range(heads):
        sl = slice(g * LANES, (g + 1) * LANES)
        qatt_ref[:, sl] = qg[g].astype(BF16)
        katt_ref[:, sl] = (kn[:, sl] + kr_rot).astype(BF16)
    vatt_ref[...] = _dot(cb, wuv_ref[...]).astype(BF16)
    ckv_ref[...] = ckv
    kr_ref[...] = pltpu.roll(kr_rot, 64, axis=1)[:, 0:32]
    for r, v in zip((hq_ref, hlf_ref, hk_ref, hv_ref, hg_ref), hg5):
        r[...] = v
    glu_ref[...] = glu


def _inproj_sample_kernel(h_ref, win_ref, qn_ref, wuq_ref, kvn_ref, cq_ref, sq_ref, ck_ref, sk_ref, lb_ref,
                          wukt_ref,
                          qatt_ref, qlat_ref, ckv_ref, kr_ref, hq_ref, hlf_ref, hk_ref, hv_ref, hg_ref,
                          glu_ref, *, ql, kv, heads, dn):
    x = h_ref[...].astype(BF16)
    qg, ckv, kr_rot, hg5, glu = _proj_common(x, win_ref, qn_ref, wuq_ref, kvn_ref, cq_ref, sq_ref, ck_ref, sk_ref,
                                             lb_ref, ql=ql, kv=kv, heads=heads)
    for g in range(heads):
        qatt_ref[:, g * LANES:(g + 1) * LANES] = qg[g].astype(BF16)
        qn = qg[g][:, 0:dn].astype(BF16)
        qlat_ref[:, g * kv:(g + 1) * kv] = _dot(qn, wukt_ref[g]).astype(BF16)
    ckv_ref[...] = ckv
    kr_ref[...] = pltpu.roll(kr_rot, 64, axis=1)[:, 0:32]
    for r, v in zip((hq_ref, hlf_ref, hk_ref, hv_ref, hg_ref), hg5):
        r[...] = v
    glu_ref[...] = glu


def _inproj(h, lw, tabs, *, tab_period, sample):
    t, d = h.shape
    ntiles = t // TM
    ql, kv, heads = lw["ql"], lw["kv"], lw["heads"]
    tab_spec = pl.BlockSpec((TM, LANES), lambda i: (i % tab_period, 0))
    in_specs = [pl.BlockSpec((TM, d), lambda i: (i, 0)),
                _full(lw["w_in"].shape), _full((1, ql)), _full(lw["w_uq"].shape), _full((1, kv)),
                tab_spec, tab_spec, tab_spec, tab_spec, _full((8, 256))]
    args = [h, lw["w_in"], lw["q_norm"], lw["w_uq"], lw["kv_norm"], *tabs, lw["lbc"]]

    def tok(n, dt):
        return jax.ShapeDtypeStruct((t, n), dt), pl.BlockSpec((TM, n), lambda i: (i, 0))

    tail = [tok(kv, F32), tok(32, F32)] + [tok(256, F32)] * 6
    if sample:
        in_specs += [_full(lw["w_ukt"].shape)]
        args += [lw["w_ukt"]]
        outs = [tok(heads * LANES, BF16), tok(heads * kv, BF16)] + tail
        body = functools.partial(_inproj_sample_kernel, ql=ql, kv=kv, heads=heads, dn=lw["dn"])
        name = "inproj_sample"
    else:
        in_specs += [_full(lw["w_uk"].shape), _full(lw["w_uv"].shape)]
        args += [lw["w_uk"], lw["w_uv"]]
        outs = [tok(heads * LANES, BF16), tok(heads * LANES, BF16), tok(lw["w_uv"].shape[1], BF16)] + tail
        body = functools.partial(_inproj_prompt_kernel, ql=ql, kv=kv, heads=heads)
        name = "inproj_prompt"
    return pl.pallas_call(
        body, grid=(ntiles,), in_specs=in_specs,
        out_specs=[o[1] for o in outs], out_shape=[o[0] for o in outs],
        compiler_params=_cp("arbitrary"), name=name)(*args)


def _attn_prompt_kernel(q_ref, k_ref, v_ref, o_ref, *, seq, tq, dv):
    lane = lax.broadcasted_iota(I32, (1, LANES), 1)
    row = lax.broadcasted_iota(I32, (tq, tq), 0)
    col = lax.broadcasted_iota(I32, (tq, tq), 1)
    causal = col <= row
    vall = v_ref[...]
    vh = [jnp.where((lane >= hh * dv) & (lane < (hh + 1) * dv), vall, jnp.zeros_like(vall)) for hh in range(2)]
    for qi in range(seq // tq):
        q0, q1 = qi * tq, (qi + 1) * tq
        out = None
        for hh in range(2):
            hs = slice(hh * LANES, (hh + 1) * LANES)
            q = q_ref[q0:q1, hs]
            sd = lax.dot_general(q, k_ref[q0:q1, hs], _NT, preferred_element_type=F32)
            sd = jnp.where(causal, sd, -jnp.inf)
            m = jnp.max(sd, axis=1, keepdims=True)
            if qi > 0:
                so = lax.dot_general(q, k_ref[0:q0, hs], _NT, preferred_element_type=F32)
                m = jnp.maximum(m, jnp.max(so, axis=1, keepdims=True))
                po = jnp.exp(so - m)
            pd = jnp.exp(sd - m)
            l = jnp.sum(pd, axis=1, keepdims=True)
            o = _dot(pd.astype(BF16), vh[hh][q0:q1, :])
            if qi > 0:
                l = l + jnp.sum(po, axis=1, keepdims=True)
                o = o + _dot(po.astype(BF16), vh[hh][0:q0, :])
            o = o / l
            out = o if out is None else out + o
        o_ref[q0:q1, :] = out.astype(o_ref.dtype)


def _attn_prompt(qatt, katt, vatt, *, batch, seq, heads, dv):
    assert heads % 2 == 0 and 2 * dv == LANES
    tq = min(512, seq)
    q3 = qatt.reshape(batch, seq, heads * LANES)
    k3 = katt.reshape(batch, seq, heads * LANES)
    v3 = vatt.reshape(batch, seq, heads * dv)
    out = pl.pallas_call(
        functools.partial(_attn_prompt_kernel, seq=seq, tq=tq, dv=dv),
        grid=(batch, heads // 2),
        in_specs=[pl.BlockSpec((None, seq, 2 * LANES), lambda b, h: (b, 0, h)),
                  pl.BlockSpec((None, seq, 2 * LANES), lambda b, h: (b, 0, h)),
                  pl.BlockSpec((None, seq, LANES), lambda b, h: (b, 0, h))],
        out_specs=pl.BlockSpec((None, seq, LANES), lambda b, h: (b, 0, h)),
        out_shape=jax.ShapeDtypeStruct((batch, seq, heads * dv), BF16),
        compiler_params=_cp("arbitrary", "arbitrary"), name="attn_prompt")(q3, k3, v3)
    return out.reshape(batch * seq, heads * dv)


def _attn_sample_kernel(pt_ref, qlat_ref, qatt_ref, cnew_ref, knew_ref, ckv_hbm, krt_hbm, o_ref, kbuf, rbuf, sem,
                        *, layer, pages, psize, nchunks, nseq, heads, tnew, dn, dr):
    b = pl.program_id(0)

    def fetch(seq, chunk, slot):
        for i in range(pages):
            page = pt_ref[seq, chunk * pages + i]
            keys = pl.ds(i * psize, psize)
            pltpu.make_async_copy(ckv_hbm.at[layer, page], kbuf.at[slot, keys], sem.at[0, slot]).start()
            pltpu.make_async_copy(krt_hbm.at[layer, page], rbuf.at[slot, :, keys], sem.at[1, slot]).start()

    def fetch_ahead(g_next):
        step_ahead, chunk = divmod(g_next, nchunks)

        def go():
            fetch(b + step_ahead, chunk, lax.rem(b * nchunks + g_next, NSLOT))
        if step_ahead == 0:
            go()
        else:
            pl.when(b + step_ahead < nseq)(go)

    @pl.when(b == 0)
    def _():
        for g0 in range(NSLOT - 1):
            fetch_ahead(g0)

    ql = qlat_ref[...]
    qr = qatt_ref[:, dn:dn + dr]
    rows, kv = ql.shape

    def wait(slot):
        pltpu.make_async_copy(kbuf.at[slot], kbuf.at[slot], sem.at[0, slot]).wait()
        pltpu.make_async_copy(rbuf.at[slot], rbuf.at[slot], sem.at[1, slot]).wait()

    def scores(slot):
        kc = kbuf[slot].astype(BF16)
        krt = rbuf[slot].astype(BF16)
        return lax.dot_general(ql, kc, _NT, preferred_element_type=F32) + _dot(qr, krt)

    slot0 = lax.rem(b * nchunks, NSLOT)
    wait(slot0)
    s_cur = scores(slot0)
    m1 = jnp.full((rows, 1), -jnp.inf, F32)
    l1 = jnp.zeros((rows, 1), F32)
    acc1 = jnp.zeros((rows, kv), F32)
    for c in range(nchunks):
        slot = lax.rem(b * nchunks + c, NSLOT)
        fetch_ahead(c + NSLOT - 1)
        if c + 1 < nchunks:
            slot_n = lax.rem(b * nchunks + c + 1, NSLOT)
            wait(slot_n)
            s_next = scores(slot_n)
        m_new = jnp.maximum(m1, jnp.max(s_cur, axis=1, keepdims=True))
        alpha = jnp.exp(m1 - m_new)
        p = jnp.exp(s_cur - m_new)
        l1 = alpha * l1 + jnp.sum(p, axis=1, keepdims=True)
        acc1 = alpha * acc1 + _dot(p.astype(BF16), kbuf[slot].astype(BF16))
        m1 = m_new
        if c + 1 < nchunks:
            s_cur = s_next

    qlf, qrf = ql.astype(F32), qr.astype(F32)
    cn, kn = cnew_ref[...], knew_ref[...]
    trow = lax.shift_right_logical(lax.broadcasted_iota(I32, (rows, 1), 0), int(math.log2(heads)))
    sn = []
    for t in range(tnew):
        st = (jnp.sum(qlf * cn[t:t + 1, :], axis=1, keepdims=True)
              + jnp.sum(qrf * kn[t:t + 1, :], axis=1, keepdims=True))
        sn.append(jnp.where(trow >= t, st, -jnp.inf))
    m2 = m1
    for t in range(tnew):
        m2 = jnp.maximum(m2, sn[t])
    a2 = jnp.exp(m1 - m2)
    l2 = a2 * l1
    acc = a2 * acc1
    for t in range(tnew):
        pt = jnp.exp(sn[t] - m2)
        l2 = l2 + pt
        acc = acc + pt * cn[t:t + 1, :]
    o_ref[...] = acc / l2


def _attn_sample(page_table, qlat, qatt, ckv_new, kr_new, cache_ckv, cache_krope, layer, *, heads, tnew, dn):
    nseq, npages = page_table.shape
    psize, kv = cache_ckv.shape[2], cache_ckv.shape[3]
    dr = cache_krope.shape[3]
    pages = math.gcd(npages, SAMPLE_PAGES)
    nchunks = npages // pages
    krope_t = jnp.swapaxes(cache_krope, 2, 3)
    rows = heads * tnew
    q3 = qlat.reshape(nseq, rows, kv)
    qa3 = qatt.reshape(nseq, rows, LANES)
    c3 = ckv_new.reshape(nseq, tnew, kv)
    k3 = kr_new.reshape(nseq, tnew, dr)
    in_specs = [pl.BlockSpec((None, rows, kv), lambda b, pt: (b, 0, 0)),
                pl.BlockSpec((None, rows, LANES), lambda b, pt: (b, 0, 0)),
                pl.BlockSpec((None, tnew, kv), lambda b, pt: (b, 0, 0)),
                pl.BlockSpec((None, tnew, dr), lambda b, pt: (b, 0, 0)),
                pl.BlockSpec(memory_space=pl.ANY), pl.BlockSpec(memory_space=pl.ANY)]
    out = pl.pallas_call(
        functools.partial(_attn_sample_kernel, layer=layer, pages=pages, psize=psize, nchunks=nchunks, nseq=nseq,
                          heads=heads, tnew=tnew, dn=dn, dr=dr),
        grid_spec=pltpu.PrefetchScalarGridSpec(
            num_scalar_prefetch=1, grid=(nseq,), in_specs=in_specs,
            out_specs=pl.BlockSpec((None, rows, kv), lambda b, pt: (b, 0, 0)),
            scratch_shapes=[pltpu.VMEM((NSLOT, pages * psize, kv), F32), pltpu.VMEM((NSLOT, dr, pages * psize), F32),
                            pltpu.SemaphoreType.DMA((2, NSLOT))]),
        out_shape=jax.ShapeDtypeStruct((nseq, rows, kv), F32),
        compiler_params=_cp("arbitrary"), name="attn_sample")(page_table, q3, qa3, c3, k3, cache_ckv, krope_t)
    return out.reshape(nseq * tnew, heads * kv)


def _head_blocks(w, dh, dtype):
    sh = int(math.log2(dh))
    r = lax.shift_right_logical(lax.broadcasted_iota(I32, (w, w), 0), sh)
    c = lax.shift_right_logical(lax.broadcasted_iota(I32, (w, w), 1), sh)
    return jnp.where(r == c, 1.0, 0.0).astype(dtype)


def _hgrn_intra_kernel(q_ref, lf_ref, k_ref, v_ref, oi_ref, qd_ref, kd_ref, ea_ref, a_sc, *, c, dh):
    tm, w = q_ref.shape
    sh = int(math.log2(c))
    r = lax.broadcasted_iota(I32, (tm, tm), 0)
    cc = lax.broadcasted_iota(I32, (tm, tm), 1)
    same = lax.shift_right_logical(r, sh) == lax.shift_right_logical(cc, sh)
    tril = jnp.where(same & (cc <= r), 1.0, 0.0).astype(BF16)
    blk = jnp.where(same, 1.0, 0.0).astype(BF16)
    lf = lf_ref[...]
    hi = lf.astype(BF16)
    r1 = lf - hi.astype(F32)
    mid = r1.astype(BF16)
    lo = (r1 - mid.astype(F32)).astype(BF16)
    a = _dot(tril, hi) + _dot(tril, mid) + _dot(tril, lo)
    atot = _dot(blk, hi) + _dot(blk, mid) + _dot(blk, lo)
    a_sc[...] = a
    qd_ref[...] = q_ref[...] * jnp.exp(a)
    kd_ref[...] = k_ref[...] * jnp.exp(atot - a)
    ea_ref[...] = jnp.exp(atot)
    bones_b = _head_blocks(w, dh, BF16)
    srow = lax.broadcasted_iota(I32, (c, 1), 0)

    def chunk(ci, carry):
        sl = pl.ds(pl.multiple_of(ci * c, c), c)
        ac, qc, kc, vc = a_sc[sl, :], q_ref[sl, :], k_ref[sl, :], v_ref[sl, :]
        ns = [8 if (c > 8 and t < 8) else c for t in range(c)]
        ws = []
        for t in range(c):
            n = ns[t]
            d = jnp.where(srow[0:n] <= t, ac[t:t + 1, :] - ac[0:n, :], -jnp.inf)
            ws.append(jnp.exp(d) * (qc[t:t + 1, :] * kc[0:n, :]))
        p = _dot(jnp.concatenate(ws, axis=0).astype(BF16), bones_b)
        outs, off = [], 0
        for t in range(c):
            outs.append(jnp.sum(p[off:off + ns[t], :] * vc[0:ns[t], :], axis=0, keepdims=True))
            off += ns[t]
        oi_ref[sl, :] = jnp.concatenate(outs, axis=0)
        return carry
    lax.fori_loop(0, tm // c, chunk, 0, unroll=2)


def _hgrn_seq_kernel(oi_ref, qd_ref, kd_ref, v_ref, g_ref, ea_ref, s0_ref, ng_ref, o_ref, sfin_ref, st_sc,
                     *, bt, nsteps, dh, unroll):
    j = pl.program_id(1)
    w = qd_ref.shape[-1]
    heads = w // dh
    bones = _head_blocks(w, dh, F32)
    bones_b = bones.astype(BF16)

    @pl.when(j == 0)
    def _():
        def init(b, carry):
            s0 = s0_ref[b]
            st_sc[b] = (jnp.concatenate([s0] * heads, axis=1) * bones).T
            return carry
        lax.fori_loop(0, bt, init, 0)

    ng = ng_ref[...]

    def step(b, carry):
        st = st_sc[b]
        g = g_ref[b]
        o = oi_ref[b] + lax.dot_general(qd_ref[b].astype(BF16), st.astype(BF16), _NT, preferred_element_type=F32)
        upd = lax.dot_general(v_ref[b].astype(BF16), kd_ref[b].astype(BF16), _TN, preferred_element_type=F32)
        st_sc[b] = st * ea_ref[b][0:1, :] + upd * bones
        ms = _dot((o * o).astype(BF16), bones_b) * (1.0 / dh)
        o_ref[b] = o * lax.rsqrt(ms + EPS) * ng * (g * jax.nn.sigmoid(g))
        return carry
    lax.fori_loop(0, bt, step, 0, unroll=unroll)

    @pl.when(j == nsteps - 1)
    def _():
        def fin(b, carry):
            sbd = st_sc[b].T
            acc = sbd[:, 0:dh]
            for h in range(1, heads):
                acc = acc + sbd[:, h * dh:(h + 1) * dh]
            sfin_ref[b] = acc
            return carry
        lax.fori_loop(0, bt, fin, 0)


def _hgrn(q, lf, k, v, g, s0, norm_g, *, c, bt, dh):
    batch, t, w = q.shape
    rows = batch * t
    tm = math.gcd(rows, TM)
    flat = lambda a: a.reshape(rows, w)
    tile = pl.BlockSpec((tm, w), lambda i: (i, 0))
    oi, qd, kd, ea = pl.pallas_call(
        functools.partial(_hgrn_intra_kernel, c=c, dh=dh),
        grid=(rows // tm,),
        in_specs=[tile] * 4, out_specs=[tile] * 4,
        out_shape=[jax.ShapeDtypeStruct((rows, w), F32)] * 4,
        scratch_shapes=[pltpu.VMEM((tm, w), F32)],
        compiler_params=_cp("arbitrary"), name="hgrn_intra")(flat(q), flat(lf), flat(k), flat(v))
    nsteps = t // c
    tok = pl.BlockSpec((bt, c, w), lambda bi, j: (bi, j, 0))
    st = pl.BlockSpec((bt, w, dh), lambda bi, j: (bi, 0, 0))
    b3 = lambda a: a.reshape(batch, t, w)
    return pl.pallas_call(
        functools.partial(_hgrn_seq_kernel, bt=bt, nsteps=nsteps, dh=dh, unroll=4),
        grid=(batch // bt, nsteps),
        in_specs=[tok] * 6 + [st, _full((1, w))],
        out_specs=[tok, st],
        out_shape=[jax.ShapeDtypeStruct((batch, t, w), F32), jax.ShapeDtypeStruct((batch, w, dh), F32)],
        scratch_shapes=[pltpu.VMEM((bt, w, w), F32)],
        compiler_params=_cp("arbitrary", "arbitrary"), name="hgrn_seq")(
            b3(oi), b3(qd), b3(kd), v, g, b3(ea), s0, norm_g)


def _conv_kernel(glu_ref, buf_ref, w_ref, cb_ref, lg_ref, lb_ref, o_ref, nbuf_ref, ext_sc, *, tt, taps, nt):
    t = pl.program_id(1)
    hist = CONV_HIST

    @pl.when(t == 0)
    def _():
        ext_sc[0:hist, :] = buf_ref[...]

    ext_sc[hist:hist + tt, :] = glu_ref[...]
    base = hist - (taps - 1)
    acc = w_ref[0:1, :] * ext_sc[base:base + tt, :]
    for j in range(1, taps):
        acc = acc + w_ref[j:j + 1, :] * ext_sc[base + j:base + j + tt, :]
    y = _ln(acc + cb_ref[...], lg_ref[...], lb_ref[...])
    o_ref[...] = y * jax.nn.sigmoid(y)

    @pl.when(t == nt - 1)
    def _():
        nbuf_ref[...] = ext_sc[tt + base:tt + hist, :]

    ext_sc[0:hist, :] = ext_sc[tt:tt + hist, :]


def _conv(glu, buf_padded, w, cb, lg, lb, *, tt):
    batch, t, ch = glu.shape
    taps = w.shape[0]
    nt = t // tt
    return pl.pallas_call(
        functools.partial(_conv_kernel, tt=tt, taps=taps, nt=nt),
        grid=(batch, nt),
        in_specs=[pl.BlockSpec((None, tt, ch), lambda b, i: (b, i, 0)),
                  pl.BlockSpec((None, CONV_HIST, ch), lambda b, i: (b, 0, 0)),
                  _full(w.shape), _full((1, ch)), _full((1, ch)), _full((1, ch))],
        out_specs=[pl.BlockSpec((None, tt, ch), lambda b, i: (b, i, 0)),
                   pl.BlockSpec((None, taps - 1, ch), lambda b, i: (b, 0, 0))],
        out_shape=[jax.ShapeDtypeStruct((batch, t, ch), F32), jax.ShapeDtypeStruct((batch, taps - 1, ch), F32)],
        scratch_shapes=[pltpu.VMEM((CONV_HIST + tt, ch), F32)],
        compiler_params=_cp("arbitrary", "arbitrary"), name="conv")(glu, buf_padded, w, cb, lg, lb)


def _outproj_kernel(hp_ref, hs_ref, oa_ref, ob_ref, oc_ref, olat_ref, obs_ref, ocs_ref, wuv_ref, wout_ref, g_ref, b_ref,
                    wr_ref, br_ref, h1_ref, route_ref, routet_ref, mix_sc, *, ntp, alpha, heads, kv, dv, ne, ng):
    i = pl.program_id(0)
    da = heads * dv
    wb = ob_ref.shape[1]

    @pl.when(i < ntp)
    def _():
        mix_sc[...] = (alpha * hp_ref[...] + _dot(oa_ref[...], wout_ref[0:da, :])
                       + _dot(ob_ref[...].astype(BF16), wout_ref[da:da + wb, :])
                       + _dot(oc_ref[...].astype(BF16), wout_ref[da + wb:, :]))

    @pl.when(i >= ntp)
    def _():
        mix = (alpha * hs_ref[...] + _dot(obs_ref[...].astype(BF16), wout_ref[da:da + wb, :])
               + _dot(ocs_ref[...].astype(BF16), wout_ref[da + wb:, :]))
        for h in range(heads):
            oh = _dot(olat_ref[:, h * kv:(h + 1) * kv].astype(BF16), wuv_ref[h])
            mix = mix + _dot(oh.astype(BF16), wout_ref[h * dv:(h + 1) * dv, :])
        mix_sc[...] = mix

    h1 = _ln(mix_sc[...], g_ref[...], b_ref[...])
    h1_ref[...] = h1
    h_hi = h1.astype(BF16)
    h_lo = (h1 - h_hi.astype(F32)).astype(BF16)
    part = _dot(h_hi, wr_ref[...])
    logits = (part[:, 0:LANES] + part[:, LANES:] + _dot(h_lo, wr_ref[:, 0:LANES])
              + br_ref[...])
    tm = logits.shape[0]
    lane = lax.broadcasted_iota(I32, (tm, LANES), 1).astype(F32)
    big = jnp.float32(1e9)
    epg = ne // ng
    is_g = (lane >= ne) & (lane < ne + ng)
    gl = jnp.where(is_g, logits, -jnp.inf)
    gmax = jnp.max(gl, axis=1, keepdims=True)
    gidx = jnp.min(jnp.where(gl == gmax, lane, big), axis=1, keepdims=True) - ne
    p_g = 1.0 / jnp.sum(jnp.where(is_g, jnp.exp(logits - gmax), 0.0), axis=1, keepdims=True)
    lo = gidx * epg
    el = jnp.where((lane >= lo) & (lane < lo + epg), logits, -jnp.inf)
    v1 = jnp.max(el, axis=1, keepdims=True)
    e1 = jnp.min(jnp.where(el == v1, lane, big), axis=1, keepdims=True)
    el2 = jnp.where(lane == e1, -jnp.inf, el)
    v2 = jnp.max(el2, axis=1, keepdims=True)
    e2 = jnp.min(jnp.where(el2 == v2, lane, big), axis=1, keepdims=True)
    tt = jnp.exp(v2 - v1)
    w1 = p_g / (1.0 + tt)
    w2 = w1 * tt
    route = jnp.where(lane == 0, e1, jnp.where(lane == 1, e2, jnp.where(lane == 2, w1, jnp.where(lane == 3, w2, 0.0))))
    route_ref[...] = route
    routet_ref[...] = route.T[0:8, :]


def _outproj(h_p, h_s, oa_p, ob_p, oc_p, olat_s, ob_s, oc_s, lw, *, ntp, nts, alpha):
    d = h_p.shape[1]
    heads, kv, dv = lw["heads"], lw["kv"], lw["dv"]
    nt = ntp + nts
    t_all = nt * TM

    def pspec(n):
        return pl.BlockSpec((TM, n), lambda i: (jnp.minimum(i, ntp - 1), 0))

    def sspec(n):
        return pl.BlockSpec((TM, n), lambda i: (jnp.maximum(i - ntp, 0), 0))

    tok = pl.BlockSpec((TM, d), lambda i: (i, 0))
    return pl.pallas_call(
        functools.partial(_outproj_kernel, ntp=ntp, alpha=alpha, heads=heads, kv=kv, dv=dv, ne=lw["ne"], ng=lw["ng"]),
        grid=(nt,),
        in_specs=[pspec(d), sspec(d), pspec(oa_p.shape[1]), pspec(ob_p.shape[1]), pspec(oc_p.shape[1]),
                  sspec(olat_s.shape[1]), sspec(ob_s.shape[1]), sspec(oc_s.shape[1]),
                  _full(lw["w_uv_h"].shape), _full(lw["w_out"].shape), _full((1, d)), _full((1, d)),
                  _full(lw["w_r"].shape), _full((1, LANES))],
        out_specs=[tok, pl.BlockSpec((TM, LANES), lambda i: (i, 0)), pl.BlockSpec((8, TM), lambda i: (0, i))],
        out_shape=[jax.ShapeDtypeStruct((t_all, d), F32), jax.ShapeDtypeStruct((t_all, LANES), F32),
                   jax.ShapeDtypeStruct((8, t_all), F32)],
        scratch_shapes=[pltpu.VMEM((TM, d), F32)],
        compiler_params=_cp("arbitrary"), name="outproj")(
            h_p, h_s, oa_p, ob_p, oc_p, olat_s, ob_s, oc_s, lw["w_uv_h"], lw["w_out"], lw["ln1_g"], lw["ln1_b"],
            lw["w_r"], lw["b_r"])


def _plan_kernel(rt_ref, dest_ref, be_ref, meta_ref, seg_ref, cnt_sc, seg_sc, carry_sc, *, bm, ne, nbp):
    p = pl.program_id(0)
    first = pl.program_id(1) == 0
    tm = rt_ref.shape[1]
    e1, e2 = rt_ref[0:1, :], rt_ref[1:2, :]
    sub = lax.broadcasted_iota(I32, (LANES, tm), 0).astype(F32)
    oh1 = sub == e1
    oh2 = sub == e2
    ohs = jnp.where(oh1, 1.0, 0.0) + jnp.where(oh2, 1.0, 0.0)
    tile_cnt = jnp.sum(ohs, axis=1, keepdims=True)

    @pl.when((p == 0) & first)
    def _():
        cnt_sc[...] = jnp.zeros_like(cnt_sc)

    @pl.when(p == 0)
    def _():
        cnt_sc[...] = cnt_sc[...] + tile_cnt

    @pl.when((p == 1) & first)
    def _():
        cnt = cnt_sc[...]
        padded = jnp.ceil(cnt * (1.0 / bm)) * bm
        r = lax.broadcasted_iota(I32, (LANES, LANES), 0)
        cidx = lax.broadcasted_iota(I32, (LANES, LANES), 1)
        lstrict = jnp.where(cidx < r, 1.0, 0.0).astype(F32)
        seg_start = _dot_hi(lstrict, jnp.broadcast_to(padded, (LANES, LANES)))[:, 0:1]
        seg_sc[...] = seg_start
        carry_sc[...] = jnp.zeros_like(carry_sc)
        seg_end = seg_start + padded
        jl = lax.broadcasted_iota(I32, (LANES, nbp), 1).astype(F32) * bm
        be = jnp.sum(jnp.where(seg_end <= jl, 1.0, 0.0), axis=0, keepdims=True)
        be_ref[...] = jnp.minimum(be, ne - 1).astype(I32)
        total = jnp.max(seg_end, axis=0, keepdims=True)
        meta_ref[...] = jnp.broadcast_to(total * (1.0 / bm), (1, LANES)).astype(I32)
        eye = r == cidx
        seg_ref[0:1, :] = jnp.sum(jnp.where(eye, seg_start, 0.0), axis=0, keepdims=True).astype(I32)
        seg_ref[1:2, :] = jnp.sum(jnp.where(eye, seg_end, 0.0), axis=0, keepdims=True).astype(I32)

    @pl.when(p == 1)
    def _():
        rr = lax.broadcasted_iota(I32, (tm, tm), 0)
        ccol = lax.broadcasted_iota(I32, (tm, tm), 1)
        ustrict = jnp.where(rr < ccol, 1.0, 0.0).astype(BF16)
        base = _dot(ohs.astype(BF16), ustrict) + carry_sc[...] + seg_sc[...]
        d1 = jnp.sum(jnp.where(oh1, base, 0.0), axis=0, keepdims=True)
        d2 = jnp.sum(jnp.where(oh2, base, 0.0), axis=0, keepdims=True)
        dest_ref[0:1, :] = d1.astype(I32)
        dest_ref[1:2, :] = d2.astype(I32)
        carry_sc[...] = carry_sc[...] + tile_cnt


def _plan(route_t, *, ne, nb_max):
    t_all = route_t.shape[1]
    tm = 512 if t_all % 512 == 0 else TM
    nt = t_all // tm
    nbp = pl.cdiv(nb_max, LANES) * LANES
    return pl.pallas_call(
        functools.partial(_plan_kernel, bm=MOE_BM, ne=ne, nbp=nbp),
        grid=(2, nt),
        in_specs=[pl.BlockSpec((8, tm), lambda p, i: (0, i))],
        out_specs=[pl.BlockSpec((2, tm), lambda p, i: (0, i * p)), _full((1, nbp)), _full((1, LANES)),
                   _full((2, LANES))],
        out_shape=[jax.ShapeDtypeStruct((2, t_all), I32), jax.ShapeDtypeStruct((1, nbp), I32),
                   jax.ShapeDtypeStruct((1, LANES), I32), jax.ShapeDtypeStruct((2, LANES), I32)],
        scratch_shapes=[pltpu.VMEM((LANES, 1), F32)] * 3,
        compiler_params=_cp("arbitrary", "arbitrary"), name="moe_plan")(route_t)


def _scatter_kernel(d1_ref, d2_ref, seg_ref, meta_ref, h_ref, rows_hbm, zbuf, sem, zsem, *, tm, ne, bm, nb):
    @pl.when(pl.program_id(0) == 0)
    def _():
        zbuf[...] = jnp.zeros_like(zbuf)
        for e in range(ne):
            @pl.when(seg_ref[1, e] > seg_ref[0, e])
            def _():
                last = pl.multiple_of(seg_ref[1, e] - bm, bm)
                pltpu.make_async_copy(zbuf, rows_hbm.at[pl.ds(last, bm)], zsem).start()

        def ztail(jb, carry):
            pltpu.make_async_copy(zbuf, rows_hbm.at[pl.ds(pl.multiple_of(jb * bm, bm), bm)], zsem).start()
            return carry
        lax.fori_loop(meta_ref[0, 0], nb, ztail, 0)
        for e in range(ne):
            @pl.when(seg_ref[1, e] > seg_ref[0, e])
            def _():
                pltpu.make_async_copy(zbuf, rows_hbm.at[pl.ds(0, bm)], zsem).wait()

        def zwait(jb, carry):
            pltpu.make_async_copy(zbuf, rows_hbm.at[pl.ds(0, bm)], zsem).wait()
            return carry
        lax.fori_loop(meta_ref[0, 0], nb, zwait, 0)

    for r in range(tm):
        src = h_ref.at[pl.ds(r, 1)]
        pltpu.make_async_copy(src, rows_hbm.at[pl.ds(d1_ref[0, 0, r], 1)], sem.at[0]).start(priority=0)
        pltpu.make_async_copy(src, rows_hbm.at[pl.ds(d2_ref[0, 0, r], 1)], sem.at[1]).start(priority=1)
    pltpu.make_async_copy(h_ref, rows_hbm.at[pl.ds(0, tm)], sem.at[0]).wait()
    pltpu.make_async_copy(h_ref, rows_hbm.at[pl.ds(0, tm)], sem.at[1]).wait()


def _scatter(h1, d1, d2, seg, meta, nrows, *, ne):
    t_all, d = h1.shape
    tm = d1.shape[2]
    dspec = pl.BlockSpec((1, 1, tm), lambda i: (i, 0, 0), memory_space=pltpu.SMEM)
    smem = pl.BlockSpec(memory_space=pltpu.SMEM)
    return pl.pallas_call(
        functools.partial(_scatter_kernel, tm=tm, ne=ne, bm=MOE_BM, nb=nrows // MOE_BM),
        grid=(t_all // tm,),
        in_specs=[dspec, dspec, smem, smem, pl.BlockSpec((tm, d), lambda i: (i, 0))],
        out_specs=pl.BlockSpec(memory_space=pl.ANY),
        out_shape=jax.ShapeDtypeStruct((nrows, d), F32),
        scratch_shapes=[pltpu.VMEM((MOE_BM, d), F32), pltpu.SemaphoreType.DMA((2,)), pltpu.SemaphoreType.DMA(())],
        compiler_params=_cp("arbitrary"), name="moe_scatter")(d1, d2, seg, meta, h1)


def _ffn_kernel(be_ref, na_ref, x_ref, wg_ref, wu_ref, wd_ref, y_ref, wgb, wub, wdb):
    i = pl.program_id(0)

    @pl.when(i < na_ref[0])
    def _():
        @pl.when((i == 0) | (be_ref[i] != be_ref[jnp.maximum(i - 1, 0)]))
        def _():
            wgb[...] = wg_ref[...].astype(BF16)
            wub[...] = wu_ref[...].astype(BF16)
            wdb[...] = wd_ref[...].astype(BF16)

        x = x_ref[...].astype(BF16)
        g = _dot(x, wgb[...])
        u = _dot(x, wub[...])
        hdn = (g * jax.nn.sigmoid(g) * u).astype(BF16)
        y_ref[...] = _dot(hdn, wdb[...])

    @pl.when(i >= na_ref[0])
    def _():
        y_ref[...] = jnp.zeros_like(y_ref)


def _ffn(rows, be, nact, w_gate, w_up, w_down, layer):
    nrows, d = rows.shape
    ff = w_gate.shape[3]
    nb = nrows // MOE_BM

    def blk(i, be_r, na_r):
        return jnp.minimum(i, na_r[0] - 1)

    def wspec(a, b):
        return pl.BlockSpec((None, None, a, b), lambda i, be_r, na_r: (layer, be_r[blk(i, be_r, na_r)], 0, 0))

    row_spec = pl.BlockSpec((MOE_BM, d), lambda i, be_r, na_r: (blk(i, be_r, na_r), 0))
    return pl.pallas_call(
        _ffn_kernel,
        grid_spec=pltpu.PrefetchScalarGridSpec(
            num_scalar_prefetch=2, grid=(nb,),
            in_specs=[row_spec, wspec(d, ff), wspec(d, ff), wspec(ff, d)],
            out_specs=pl.BlockSpec((MOE_BM, d), lambda i, be_r, na_r: (i, 0)),
            scratch_shapes=[pltpu.VMEM((d, ff), BF16), pltpu.VMEM((d, ff), BF16), pltpu.VMEM((ff, d), BF16)]),
        out_shape=jax.ShapeDtypeStruct((nrows, d), F32),
        compiler_params=_cp("arbitrary"), name="moe_ffn")(be, nact, rows, w_gate, w_up, w_down)


def _combine_kernel(d1_ref, d2_ref, d1n_ref, d2n_ref, h1_ref, route_ref, yb_hbm, g_ref, b_ref, outp_ref, outs_ref,
                    buf1, buf2, sem, *, tm, nt, ntp, alpha):
    i = pl.program_id(0)
    slot = lax.rem(i, 2)

    def gather(da_ref, db_ref, s):
        def start(r, carry):
            pltpu.make_async_copy(yb_hbm.at[pl.ds(da_ref[0, 0, r], 1)], buf1.at[s, pl.ds(r, 1)],
                                  sem.at[0, s]).start(priority=0)
            pltpu.make_async_copy(yb_hbm.at[pl.ds(db_ref[0, 0, r], 1)], buf2.at[s, pl.ds(r, 1)],
                                  sem.at[1, s]).start(priority=1)
            return carry
        lax.fori_loop(0, tm, start, 0, unroll=8)

    @pl.when(i == 0)
    def _():
        gather(d1_ref, d2_ref, 0)

    @pl.when(i + 1 < nt)
    def _():
        gather(d1n_ref, d2n_ref, 1 - slot)

    pltpu.make_async_copy(yb_hbm.at[pl.ds(0, tm)], buf1.at[slot], sem.at[0, slot]).wait()
    pltpu.make_async_copy(yb_hbm.at[pl.ds(0, tm)], buf2.at[slot], sem.at[1, slot]).wait()
    moe = route_ref[:, 2:3] * buf1[slot] + route_ref[:, 3:4] * buf2[slot]
    out = _ln(alpha * h1_ref[...] + moe, g_ref[...], b_ref[...])

    @pl.when(i < ntp)
    def _():
        outp_ref[...] = out

    @pl.when(i >= ntp)
    def _():
        outs_ref[...] = out


def _combine(h1, route, yb, d1, d2, g, b, *, ntp, alpha):
    t_all, d = h1.shape
    tm = d1.shape[2]
    nt = t_all // tm
    dspec = pl.BlockSpec((1, 1, tm), lambda i: (i, 0, 0), memory_space=pltpu.SMEM)
    dnext = pl.BlockSpec((1, 1, tm), lambda i: (jnp.minimum(i + 1, nt - 1), 0, 0), memory_space=pltpu.SMEM)
    tok = pl.BlockSpec((tm, d), lambda i: (i, 0))
    return pl.pallas_call(
        functools.partial(_combine_kernel, tm=tm, nt=nt, ntp=ntp, alpha=alpha),
        grid=(nt,),
        in_specs=[dspec, dspec, dnext, dnext, tok, pl.BlockSpec((tm, LANES), lambda i: (i, 0)),
                  pl.BlockSpec(memory_space=pl.ANY), _full((1, d)), _full((1, d))],
        out_specs=[pl.BlockSpec((tm, d), lambda i: (jnp.minimum(i, ntp - 1), 0)),
                   pl.BlockSpec((tm, d), lambda i: (jnp.maximum(i - ntp, 0), 0))],
        out_shape=[jax.ShapeDtypeStruct((ntp * tm, d), F32), jax.ShapeDtypeStruct(((nt - ntp) * tm, d), F32)],
        scratch_shapes=[pltpu.VMEM((2, tm, d), F32), pltpu.VMEM((2, tm, d), F32), pltpu.SemaphoreType.DMA((2, 2))],
        compiler_params=_cp("arbitrary"), name="moe_combine")(d1, d2, d1, d2, h1, route, yb, g, b)


def _rope_tables(pos, dr, scale):
    half = dr // 2
    inv = ROPE_BASE ** (-jnp.arange(half, dtype=F32) / half)
    ang = pos.astype(F32)[:, None] * inv[None, :]
    cos, sin = jnp.cos(ang), jnp.sin(ang)
    n = pos.shape[0]
    z64, z32, o64 = jnp.zeros((n, 64), F32), jnp.zeros((n, 32), F32), jnp.ones((n, 64), F32)
    cq = jnp.concatenate([o64, cos, cos, z32], axis=1) * scale
    sq = jnp.concatenate([z64, -sin, sin, z32], axis=1) * scale
    ck = jnp.concatenate([z64, cos, cos, z32], axis=1)
    sk = jnp.concatenate([z64, -sin, sin, z32], axis=1)
    return cq, sq, ck, sk


def _prep_layer(l, p):
    w_in = p["w_in"][l]
    d = w_in.shape[0]
    ql, kv = p["mla_q_norm"].shape[1], p["mla_kv_norm"].shape[1]
    heads, dn = p["mla_w_uk"].shape[2], p["mla_w_uk"].shape[3]
    dv = p["mla_w_uv"].shape[3]
    dr = p["mla_w_uq"].shape[2] // heads - dn
    assert dn == 64 and dr == 32 and dv == 64 and ql % LANES == 0 and kv % LANES == 0
    hw = p["hgrn_lb_logits"].shape[1]
    ch = p["conv_b"].shape[1]
    assert hw == 256 and ch == 256
    o = ql + kv
    kr = w_in[:, o:o + dr]
    x1, x2 = kr[:, :dr // 2], kr[:, dr // 2:]
    seg = jnp.concatenate([jnp.zeros((d, 64), F32), x1, x2, x2, x1], axis=1)
    w_in_p = jnp.concatenate([w_in[:, :o], seg, w_in[:, o + dr:]], axis=1).astype(BF16)
    uq = p["mla_w_uq"][l].reshape(ql, heads, dn + dr)
    u1, u2 = uq[:, :, dn:dn + dr // 2], uq[:, :, dn + dr // 2:]
    w_uq = jnp.concatenate([uq[:, :, :dn], u1, u2, u2, u1], axis=2).reshape(ql, heads * LANES).astype(BF16)
    uk = p["mla_w_uk"][l]
    w_uk = jnp.concatenate([uk, jnp.zeros_like(uk)], axis=2).reshape(kv, heads * LANES).astype(BF16)
    w_ukt = jnp.transpose(uk, (1, 2, 0)).astype(BF16)
    uv = p["mla_w_uv"][l]
    lb = p["lb_all"][l]
    lbc = jnp.zeros((8, hw), F32).at[0].set(jnp.log(lb)).at[1].set(jnp.log1p(-lb)).at[2].set(1.0 - lb)
    ne, ng = p["router_e_w"].shape[2], p["router_g_w"].shape[2]
    w_r = jnp.concatenate([p["router_e_w"][l], p["router_g_w"][l], jnp.zeros((d, LANES - ne - ng), F32)], axis=1)
    b_r = jnp.concatenate([p["router_e_b"][l], p["router_g_b"][l], jnp.zeros((LANES - ne - ng,), F32)])[None, :]
    w_r_hi = w_r.astype(BF16)
    w_r = jnp.concatenate([w_r_hi, (w_r - w_r_hi.astype(F32)).astype(BF16)], axis=1)
    return dict(
        ql=ql, kv=kv, heads=heads, dn=dn, dv=dv, dr=dr, ne=ne, ng=ng,
        w_in=w_in_p, q_norm=p["mla_q_norm"][l][None, :], w_uq=w_uq, kv_norm=p["mla_kv_norm"][l][None, :],
        w_uk=w_uk, w_ukt=w_ukt, w_uv=uv.reshape(kv, heads * dv).astype(BF16),
        w_uv_h=jnp.transpose(uv, (1, 0, 2)).astype(BF16), lbc=lbc,
        hgrn_norm=jnp.tile(p["hgrn_norm"][l], hw // p["hgrn_norm"].shape[1])[None, :],
        conv_w=p["conv_w"][l], conv_b=p["conv_b"][l][None, :],
        conv_ln_g=p["conv_ln_g"][l][None, :], conv_ln_b=p["conv_ln_b"][l][None, :],
        w_out=p["w_out"][l].astype(BF16), ln1_g=p["ln1_g"][l][None, :], ln1_b=p["ln1_b"][l][None, :],
        ln2_g=p["ln2_g"][l][None, :], ln2_b=p["ln2_b"][l][None, :], w_r=w_r, b_r=b_r)


def kernel(x_prompt, x_sample, cache_ckv, cache_krope, state_hgrn, state_conv, page_table, w_in, mla_q_norm, mla_w_uq, mla_kv_norm, mla_w_uk, mla_w_uv, hgrn_lb_logits, hgrn_norm, conv_w, conv_b, conv_ln_g, conv_ln_b, w_out, ln1_g, ln1_b, ln2_g, ln2_b, router_g_w, router_g_b, router_e_w, router_e_b, exp_w_gate, exp_w_up, exp_w_down):
    batch, seq, d = x_prompt.shape
    nseq, tnew, _ = x_sample.shape
    depth = w_in.shape[0]
    t_p, t_s = batch * seq, nseq * tnew
    t_all = t_p + t_s
    assert t_p % TM == 0 and t_s % TM == 0 and seq % TM == 0 and seq % HGRN_C == 0
    ntp, nts = t_p // TM, t_s // TM
    past_len = page_table.shape[1] * cache_ckv.shape[2]
    alpha = (2 * depth) ** 0.25
    hb, dkb = state_hgrn.shape[2], state_hgrn.shape[3]
    taps = conv_w.shape[1]
    assert taps - 1 <= CONV_HIST

    lbp = jax.nn.softmax(hgrn_lb_logits.astype(F32), axis=0)
    lbcs = jnp.cumsum(lbp, axis=0)
    params = dict(w_in=w_in, mla_q_norm=mla_q_norm, mla_w_uq=mla_w_uq, mla_kv_norm=mla_kv_norm, mla_w_uk=mla_w_uk,
                  mla_w_uv=mla_w_uv, hgrn_lb_logits=hgrn_lb_logits, lb_all=lbcs - lbcs[0:1], hgrn_norm=hgrn_norm,
                  conv_w=conv_w, conv_b=conv_b, conv_ln_g=conv_ln_g, conv_ln_b=conv_ln_b, w_out=w_out,
                  ln1_g=ln1_g, ln1_b=ln1_b, ln2_g=ln2_g, ln2_b=ln2_b, router_g_w=router_g_w, router_g_b=router_g_b,
                  router_e_w=router_e_w, router_e_b=router_e_b)
    dn = mla_w_uk.shape[3]
    dr = cache_krope.shape[3]
    scale = (dn + dr) ** -0.5
    tabs_p = _rope_tables(jnp.arange(seq), dr, scale)
    tabs_s = _rope_tables(jnp.tile(past_len + jnp.arange(tnew), nseq), dr, scale)

    ne = router_e_w.shape[2]
    n_assign = 2 * t_all
    nb_max = (n_assign + ne * (MOE_BM - 1) + MOE_BM - 1) // MOE_BM
    nrows = nb_max * MOE_BM

    h_p, h_s = x_prompt.reshape(t_p, d), x_sample.reshape(t_s, d)
    outs = {k: [] for k in ("ckv_p", "kr_p", "sh_p", "sc_p", "ckv_s", "kr_s", "sh_s", "sc_s")}
    hw = hb * dkb
    tpad = 8
    bt_s = math.gcd(nseq, 16)
    for l in range(depth):
        lw = _prep_layer(l, params)
        heads, kv, dv = lw["heads"], lw["kv"], lw["dv"]
        (qatt, katt, vatt, ckv_p, kr_p, hq, hlf, hk, hv, hg, glu) = _inproj(
            h_p, lw, tabs_p, tab_period=seq // TM, sample=False)
        oa_p = _attn_prompt(qatt, katt, vatt, batch=batch, seq=seq, heads=heads, dv=dv)
        r3 = lambda a: a.reshape(batch, seq, a.shape[1])
        ob_p, s_p = _hgrn(r3(hq), r3(hlf), r3(hk), r3(hv), r3(hg), jnp.zeros((batch, hw, dkb), F32), lw["hgrn_norm"],
                          c=HGRN_C, bt=batch, dh=dkb)
        oc_p, buf_p = _conv(r3(glu), jnp.zeros((batch, CONV_HIST, glu.shape[1]), F32), lw["conv_w"], lw["conv_b"],
                            lw["conv_ln_g"], lw["conv_ln_b"], tt=TM)
        (qatt_s, qlat_s, ckv_s, kr_s, hq, hlf, hk, hv, hg, glu_s) = _inproj(
            h_s, lw, tabs_s, tab_period=nts, sample=True)
        olat_s = _attn_sample(page_table, qlat_s, qatt_s, ckv_s, kr_s, cache_ckv, cache_krope, l,
                              heads=heads, tnew=tnew, dn=lw["dn"])
        r3s = lambda a: jnp.pad(a.reshape(nseq, tnew, a.shape[1]), ((0, 0), (0, tpad - tnew), (0, 0)))
        ob_s, s_s = _hgrn(r3s(hq), r3s(hlf), r3s(hk), r3s(hv), r3s(hg), state_hgrn[l].reshape(nseq, hw, dkb),
                          lw["hgrn_norm"], c=tpad, bt=bt_s, dh=dkb)
        ob_s = ob_s[:, :tnew].reshape(t_s, hw)
        bufpad = jnp.pad(state_conv[l], ((0, 0), (CONV_HIST - (taps - 1), 0), (0, 0)))
        oc_s, buf_s = _conv(glu_s.reshape(nseq, tnew, -1), bufpad, lw["conv_w"], lw["conv_b"],
                            lw["conv_ln_g"], lw["conv_ln_b"], tt=tnew)
        h1, route, route_t = _outproj(h_p, h_s, oa_p, ob_p.reshape(t_p, hw), oc_p.reshape(t_p, -1), olat_s, ob_s,
                                      oc_s.reshape(t_s, -1), lw, ntp=ntp, nts=nts, alpha=alpha)
        dest, be, meta, seg = _plan(route_t, ne=ne, nb_max=nb_max)
        d1 = dest[0].reshape(t_all // TM, 1, TM)
        d2 = dest[1].reshape(t_all // TM, 1, TM)
        rows = _scatter(h1, d1, d2, seg, meta, nrows, ne=ne)
        yb = _ffn(rows, be.reshape(-1), meta[0, 0:1], exp_w_gate, exp_w_up, exp_w_down, l)
        h_p, h_s = _combine(h1, route, yb, d1, d2, lw["ln2_g"], lw["ln2_b"], ntp=ntp, alpha=alpha)

        outs["ckv_p"].append(ckv_p.reshape(batch, seq, kv))
        outs["kr_p"].append(kr_p.reshape(batch, seq, dr))
        outs["sh_p"].append(s_p.reshape(batch, hb, dkb, dkb))
        outs["sc_p"].append(buf_p)
        outs["ckv_s"].append(ckv_s.reshape(nseq, tnew, kv))
        outs["kr_s"].append(kr_s.reshape(nseq, tnew, dr))
        outs["sh_s"].append(s_s.reshape(nseq, hb, dkb, dkb))
        outs["sc_s"].append(buf_s)
    st = lambda k: jnp.stack(outs[k])
    return (h_p.reshape(batch, seq, d), h_s.reshape(nseq, tnew, d),
            st("ckv_p"), st("kr_p"), st("sh_p"), st("sc_p"), st("ckv_s"), st("kr_s"), st("sh_s"), st("sc_s"))
```

```python
import functools
import math

import jax
import jax.numpy as jnp
from jax import lax
from jax.experimental import pallas as pl
from jax.experimental.pallas import tpu as pltpu

F32 = jnp.float32
BF16 = jnp.bfloat16
I32 = jnp.int32

LANES = 128
ROPE_BASE = 10000.0
EPS = 1e-6
TM = 256
MOE_BM = 512
HGRN_C = 16
CONV_HIST = 32
SAMPLE_PAGES = 32
NSLOT = 3
VMEM_LIMIT = 48 * 1024 * 1024
_HI = lax.Precision.HIGHEST
_NT = (((1,), (1,)), ((), ()))
_TN = (((0,), (0,)), ((), ()))


def _cp(*sem):
    return pltpu.CompilerParams(dimension_semantics=sem, vmem_limit_bytes=VMEM_LIMIT)


def _dot(a, b):
    return jnp.dot(a, b, preferred_element_type=F32)


def _dot_hi(a, b):
    return jnp.dot(a, b, preferred_element_type=F32, precision=_HI)


def _rms(x, g):
    return x * lax.rsqrt(jnp.mean(x * x, axis=-1, keepdims=True) + EPS) * g


def _ln(x, g, b):
    mu = jnp.mean(x, axis=-1, keepdims=True)
    xc = x - mu
    var = jnp.mean(xc * xc, axis=-1, keepdims=True)
    return xc * lax.rsqrt(var + EPS) * g + b


def _full(shape):
    n = len(shape)
    return pl.BlockSpec(shape, lambda *_: (0,) * n)


def _proj_common(x, win_ref, qn_ref, wuq_ref, kvn_ref, cq_ref, sq_ref, ck_ref, sk_ref, lb_ref, *, ql, kv, heads):
    o_kr = ql + kv
    o_h = o_kr + LANES
    o_c = o_h + 4 * 256
    y = _rms(_dot(x, win_ref[:, 0:ql]), qn_ref[...]).astype(BF16)
    q = _dot(y, wuq_ref[...])
    cqt, sqt = cq_ref[...], sq_ref[...]
    qg = []
    for g in range(heads):
        t = q[:, g * LANES:(g + 1) * LANES]
        qg.append(t * cqt + pltpu.roll(t, 96, axis=1) * sqt)
    ckv = _rms(_dot(x, win_ref[:, ql:o_kr]), kvn_ref[...])
    kr = _dot(x, win_ref[:, o_kr:o_h])
    kr_rot = kr * ck_ref[...] + pltpu.roll(kr, 96, axis=1) * sk_ref[...]
    uh = _dot(x, win_ref[:, o_h:o_c])
    hq, z, hv, hg = uh[:, 0:256], uh[:, 256:512], uh[:, 512:768], uh[:, 768:1024]
    log_lb, log1m_lb, one_m_lb = lb_ref[0:1, :], lb_ref[1:2, :], lb_ref[2:3, :]
    log_sig = jnp.minimum(z, 0.0) - jnp.log1p(jnp.exp(-jnp.abs(z)))
    b = log1m_lb + log_sig
    hlf = jnp.maximum(log_lb, b) + jnp.log1p(jnp.exp(-jnp.abs(log_lb - b)))
    hk = one_m_lb * jax.nn.sigmoid(-z)
    uc = _dot(x, win_ref[:, o_c:o_c + 512])
    glu = uc[:, 0:256] * jax.nn.sigmoid(uc[:, 256:512])
    return qg, ckv, kr_rot, (hq, hlf, hk, hv, hg), glu


def _inproj_prompt_kernel(h_ref, win_ref, qn_ref, wuq_ref, kvn_ref, cq_ref, sq_ref, ck_ref, sk_ref, lb_ref,
                          wuk_ref, wuv_ref,
                          qatt_ref, katt_ref, vatt_ref, ckv_ref, kr_ref, hq_ref, hlf_ref, hk_ref, hv_ref, hg_ref,
                          glu_ref, *, ql, kv, heads):
    x = h_ref[...].astype(BF16)
    qg, ckv, kr_rot, hg5, glu = _proj_common(x, win_ref, qn_ref, wuq_ref, kvn_ref, cq_ref, sq_ref, ck_ref, sk_ref,
                                             lb_ref, ql=ql, kv=kv, heads=heads)
    cb = ckv.astype(BF16)
    kn = _dot(cb, wuk_ref[...])
    for g in range(heads):
        sl = slice(g * LANES, (g + 1) * LANES)
        qatt_ref[:, sl] = qg[g].astype(BF16)
        katt_ref[:, sl] = (kn[:, sl] + kr_rot).astype(BF16)
    vatt_ref[...] = _dot(cb, wuv_ref[...]).astype(BF16)
    ckv_ref[...] = ckv
    kr_ref[...] = pltpu.roll(kr_rot, 64, axis=1)[:, 0:32]
    for r, v in zip((hq_ref, hlf_ref, hk_ref, hv_ref, hg_ref), hg5):
        r[...] = v
    glu_ref[...] = glu


def _inproj_sample_kernel(h_ref, win_ref, qn_ref, wuq_ref, kvn_ref, cq_ref, sq_ref, ck_ref, sk_ref, lb_ref,
                          wukt_ref,
                          qatt_ref, qlat_ref, ckv_ref, kr_ref, hq_ref, hlf_ref, hk_ref, hv_ref, hg_ref,
                          glu_ref, *, ql, kv, heads, dn):
    x = h_ref[...].astype(BF16)
    qg, ckv, kr_rot, hg5, glu = _proj_common(x, win_ref, qn_ref, wuq_ref, kvn_ref, cq_ref, sq_ref, ck_ref, sk_ref,
                                             lb_ref, ql=ql, kv=kv, heads=heads)
    for g in range(heads):
        qatt_ref[:, g * LANES:(g + 1) * LANES] = qg[g].astype(BF16)
        qn = qg[g][:, 0:dn].astype(BF16)
        qlat_ref[:, g * kv:(g + 1) * kv] = _dot(qn, wukt_ref[g]).astype(BF16)
    ckv_ref[...] = ckv
    kr_ref[...] = pltpu.roll(kr_rot, 64, axis=1)[:, 0:32]
    for r, v in zip((hq_ref, hlf_ref, hk_ref, hv_ref, hg_ref), hg5):
        r[...] = v
    glu_ref[...] = glu


def _inproj(h, lw, tabs, *, tab_period, sample):
    t, d = h.shape
    ntiles = t // TM
    ql, kv, heads = lw["ql"], lw["kv"], lw["heads"]
    tab_spec = pl.BlockSpec((TM, LANES), lambda i: (i % tab_period, 0))
    in_specs = [pl.BlockSpec((TM, d), lambda i: (i, 0)),
                _full(lw["w_in"].shape), _full((1, ql)), _full(lw["w_uq"].shape), _full((1, kv)),
                tab_spec, tab_spec, tab_spec, tab_spec, _full((8, 256))]
    args = [h, lw["w_in"], lw["q_norm"], lw["w_uq"], lw["kv_norm"], *tabs, lw["lbc"]]

    def tok(n, dt):
        return jax.ShapeDtypeStruct((t, n), dt), pl.BlockSpec((TM, n), lambda i: (i, 0))

    tail = [tok(kv, F32), tok(32, F32)] + [tok(256, F32)] * 6
    if sample:
        in_specs += [_full(lw["w_ukt"].shape)]
        args += [lw["w_ukt"]]
        outs = [tok(heads * LANES, BF16), tok(heads * kv, BF16)] + tail
        body = functools.partial(_inproj_sample_kernel, ql=ql, kv=kv, heads=heads, dn=lw["dn"])
        name = "inproj_sample"
    else:
        in_specs += [_full(lw["w_uk"].shape), _full(lw["w_uv"].shape)]
        args += [lw["w_uk"], lw["w_uv"]]
        outs = [tok(heads * LANES, BF16), tok(heads * LANES, BF16), tok(lw["w_uv"].shape[1], BF16)] + tail
        body = functools.partial(_inproj_prompt_kernel, ql=ql, kv=kv, heads=heads)
        name = "inproj_prompt"
    return pl.pallas_call(
        body, grid=(ntiles,), in_specs=in_specs,
        out_specs=[o[1] for o in outs], out_shape=[o[0] for o in outs],
        compiler_params=_cp("arbitrary"), name=name)(*args)


def _attn_prompt_kernel(q_ref, k_ref, v_ref, o_ref, *, seq, tq, dv):
    lane = lax.broadcasted_iota(I32, (1, LANES), 1)
    row = lax.broadcasted_iota(I32, (tq, tq), 0)
    col = lax.broadcasted_iota(I32, (tq, tq), 1)
    causal = col <= row
    vall = v_ref[...]
    vh = [jnp.where((lane >= hh * dv) & (lane < (hh + 1) * dv), vall, jnp.zeros_like(vall)) for hh in range(2)]
    for qi in range(seq // tq):
        q0, q1 = qi * tq, (qi + 1) * tq
        out = None
        for hh in range(2):
            hs = slice(hh * LANES, (hh + 1) * LANES)
            q = q_ref[q0:q1, hs]
            sd = lax.dot_general(q, k_ref[q0:q1, hs], _NT, preferred_element_type=F32)
            sd = jnp.where(causal, sd, -jnp.inf)
            m = jnp.max(sd, axis=1, keepdims=True)
            if qi > 0:
                so = lax.dot_general(q, k_ref[0:q0, hs], _NT, preferred_element_type=F32)
                m = jnp.maximum(m, jnp.max(so, axis=1, keepdims=True))
                po = jnp.exp(so - m)
            pd = jnp.exp(sd - m)
            l = jnp.sum(pd, axis=1, keepdims=True)
            o = _dot(pd.astype(BF16), vh[hh][q0:q1, :])
            if qi > 0:
                l = l + jnp.sum(po, axis=1, keepdims=True)
                o = o + _dot(po.astype(BF16), vh[hh][0:q0, :])
            o = o / l
            out = o if out is None else out + o
        o_ref[q0:q1, :] = out.astype(o_ref.dtype)


def _attn_prompt(qatt, katt, vatt, *, batch, seq, heads, dv):
    assert heads % 2 == 0 and 2 * dv == LANES
    tq = min(512, seq)
    q3 = qatt.reshape(batch, seq, heads * LANES)
    k3 = katt.reshape(batch, seq, heads * LANES)
    v3 = vatt.reshape(batch, seq, heads * dv)
    out = pl.pallas_call(
        functools.partial(_attn_prompt_kernel, seq=seq, tq=tq, dv=dv),
        grid=(batch, heads // 2),
        in_specs=[pl.BlockSpec((None, seq, 2 * LANES), lambda b, h: (b, 0, h)),
                  pl.BlockSpec((None, seq, 2 * LANES), lambda b, h: (b, 0, h)),
                  pl.BlockSpec((None, seq, LANES), lambda b, h: (b, 0, h))],
        out_specs=pl.BlockSpec((None, seq, LANES), lambda b, h: (b, 0, h)),
        out_shape=jax.ShapeDtypeStruct((batch, seq, heads * dv), BF16),
        compiler_params=_cp("arbitrary", "arbitrary"), name="attn_prompt")(q3, k3, v3)
    return out.reshape(batch * seq, heads * dv)


def _attn_sample_kernel(pt_ref, qlat_ref, qatt_ref, cnew_ref, knew_ref, ckv_hbm, krt_hbm, o_ref, kbuf, rbuf, sem,
                        *, layer, pages, psize, nchunks, nseq, heads, tnew, dn, dr):
    b = pl.program_id(0)

    def fetch(seq, chunk, slot):
        for i in range(pages):
            page = pt_ref[seq, chunk * pages + i]
            keys = pl.ds(i * psize, psize)
            pltpu.make_async_copy(ckv_hbm.at[layer, page], kbuf.at[slot, keys], sem.at[0, slot]).start()
            pltpu.make_async_copy(krt_hbm.at[layer, page], rbuf.at[slot, :, keys], sem.at[1, slot]).start()

    unguarded = nchunks >= NSLOT - 1

    def fetch_ahead(g_next):
        step_ahead, chunk = divmod(g_next, nchunks)
        slot = lax.rem(b * nchunks + g_next, NSLOT)
        if step_ahead == 0:
            fetch(b, chunk, slot)
        elif unguarded:
            fetch(jnp.minimum(b + step_ahead, nseq - 1), chunk, slot)
        else:
            pl.when(b + step_ahead < nseq)(lambda: fetch(b + step_ahead, chunk, slot))

    @pl.when(b == 0)
    def _():
        for g0 in range(NSLOT - 1):
            fetch_ahead(g0)

    ql = qlat_ref[...]
    qr = qatt_ref[:, dn:dn + dr]
    rows, kv = ql.shape

    def wait(slot):
        pltpu.make_async_copy(kbuf.at[slot], kbuf.at[slot], sem.at[0, slot]).wait()
        pltpu.make_async_copy(rbuf.at[slot], rbuf.at[slot], sem.at[1, slot]).wait()

    def scores(slot):
        kc = kbuf[slot].astype(BF16)
        krt = rbuf[slot].astype(BF16)
        return lax.dot_general(ql, kc, _NT, preferred_element_type=F32) + _dot(qr, krt)

    slot0 = lax.rem(b * nchunks, NSLOT)
    wait(slot0)
    s_cur = scores(slot0)
    m1 = jnp.full((rows, 1), -jnp.inf, F32)
    l1 = jnp.zeros((rows, 1), F32)
    acc1 = jnp.zeros((rows, kv), F32)
    for c in range(nchunks):
        slot = lax.rem(b * nchunks + c, NSLOT)
        fetch_ahead(c + NSLOT - 1)
        if c + 1 < nchunks:
            slot_n = lax.rem(b * nchunks + c + 1, NSLOT)
            wait(slot_n)
            s_next = scores(slot_n)
        m_new = jnp.maximum(m1, jnp.max(s_cur, axis=1, keepdims=True))
        alpha = jnp.exp(m1 - m_new)
        p = jnp.exp(s_cur - m_new)
        l1 = alpha * l1 + jnp.sum(p, axis=1, keepdims=True)
        acc1 = alpha * acc1 + _dot(p.astype(BF16), kbuf[slot].astype(BF16))
        m1 = m_new
        if c + 1 < nchunks:
            s_cur = s_next

    if unguarded:
        @pl.when(b == nseq - 1)
        def _():
            for g_extra in range(nchunks, nchunks + NSLOT - 1):
                wait(lax.rem(b * nchunks + g_extra, NSLOT))

    qlf, qrf = ql.astype(F32), qr.astype(F32)
    cn, kn = cnew_ref[...], knew_ref[...]
    trow = lax.shift_right_logical(lax.broadcasted_iota(I32, (rows, 1), 0), int(math.log2(heads)))
    sn = []
    for t in range(tnew):
        st = (jnp.sum(qlf * cn[t:t + 1, :], axis=1, keepdims=True)
              + jnp.sum(qrf * kn[t:t + 1, :], axis=1, keepdims=True))
        sn.append(jnp.where(trow >= t, st, -jnp.inf))
    m2 = m1
    for t in range(tnew):
        m2 = jnp.maximum(m2, sn[t])
    a2 = jnp.exp(m1 - m2)
    l2 = a2 * l1
    acc = a2 * acc1
    for t in range(tnew):
        pt = jnp.exp(sn[t] - m2)
        l2 = l2 + pt
        acc = acc + pt * cn[t:t + 1, :]
    o_ref[...] = acc / l2


def _attn_sample(page_table, qlat, qatt, ckv_new, kr_new, cache_ckv, cache_krope, layer, *, heads, tnew, dn):
    nseq, npages = page_table.shape
    psize, kv = cache_ckv.shape[2], cache_ckv.shape[3]
    dr = cache_krope.shape[3]
    pages = math.gcd(npages, SAMPLE_PAGES)
    nchunks = npages // pages
    krope_t = jnp.swapaxes(cache_krope, 2, 3)
    rows = heads * tnew
    q3 = qlat.reshape(nseq, rows, kv)
    qa3 = qatt.reshape(nseq, rows, LANES)
    c3 = ckv_new.reshape(nseq, tnew, kv)
    k3 = kr_new.reshape(nseq, tnew, dr)
    in_specs = [pl.BlockSpec((None, rows, kv), lambda b, pt: (b, 0, 0)),
                pl.BlockSpec((None, rows, LANES), lambda b, pt: (b, 0, 0)),
                pl.BlockSpec((None, tnew, kv), lambda b, pt: (b, 0, 0)),
                pl.BlockSpec((None, tnew, dr), lambda b, pt: (b, 0, 0)),
                pl.BlockSpec(memory_space=pl.ANY), pl.BlockSpec(memory_space=pl.ANY)]
    out = pl.pallas_call(
        functools.partial(_attn_sample_kernel, layer=layer, pages=pages, psize=psize, nchunks=nchunks, nseq=nseq,
                          heads=heads, tnew=tnew, dn=dn, dr=dr),
        grid_spec=pltpu.PrefetchScalarGridSpec(
            num_scalar_prefetch=1, grid=(nseq,), in_specs=in_specs,
            out_specs=pl.BlockSpec((None, rows, kv), lambda b, pt: (b, 0, 0)),
            scratch_shapes=[pltpu.VMEM((NSLOT, pages * psize, kv), F32), pltpu.VMEM((NSLOT, dr, pages * psize), F32),
                            pltpu.SemaphoreType.DMA((2, NSLOT))]),
        out_shape=jax.ShapeDtypeStruct((nseq, rows, kv), F32),
        compiler_params=_cp("arbitrary"), name="attn_sample")(page_table, q3, qa3, c3, k3, cache_ckv, krope_t)
    return out.reshape(nseq * tnew, heads * kv)


def _head_blocks(w, dh, dtype):
    sh = int(math.log2(dh))
    r = lax.shift_right_logical(lax.broadcasted_iota(I32, (w, w), 0), sh)
    c = lax.shift_right_logical(lax.broadcasted_iota(I32, (w, w), 1), sh)
    return jnp.where(r == c, 1.0, 0.0).astype(dtype)


def _hgrn_intra_kernel(q_ref, lf_ref, k_ref, v_ref, oi_ref, qd_ref, kd_ref, ea_ref, a_sc, *, c, dh):
    tm, w = q_ref.shape
    sh = int(math.log2(c))
    r = lax.broadcasted_iota(I32, (tm, tm), 0)
    cc = lax.broadcasted_iota(I32, (tm, tm), 1)
    same = lax.shift_right_logical(r, sh) == lax.shift_right_logical(cc, sh)
    tril = jnp.where(same & (cc <= r), 1.0, 0.0).astype(BF16)
    blk = jnp.where(same, 1.0, 0.0).astype(BF16)
    lf = lf_ref[...]
    hi = lf.astype(BF16)
    r1 = lf - hi.astype(F32)
    mid = r1.astype(BF16)
    lo = (r1 - mid.astype(F32)).astype(BF16)
    a = _dot(tril, hi) + _dot(tril, mid) + _dot(tril, lo)
    atot = _dot(blk, hi) + _dot(blk, mid) + _dot(blk, lo)
    a_sc[...] = a
    qd_ref[...] = q_ref[...] * jnp.exp(a)
    kd_ref[...] = k_ref[...] * jnp.exp(atot - a)
    ea_ref[...] = jnp.exp(atot)
    bones_b = _head_blocks(w, dh, BF16)
    srow = lax.broadcasted_iota(I32, (c, 1), 0)

    def chunk(ci, carry):
        sl = pl.ds(pl.multiple_of(ci * c, c), c)
        ac, qc, kc, vc = a_sc[sl, :], q_ref[sl, :], k_ref[sl, :], v_ref[sl, :]
        ns = [8 if (c > 8 and t < 8) else c for t in range(c)]
        ws = []
        for t in range(c):
            n = ns[t]
            d = jnp.where(srow[0:n] <= t, ac[t:t + 1, :] - ac[0:n, :], -jnp.inf)
            ws.append(jnp.exp(d) * (qc[t:t + 1, :] * kc[0:n, :]))
        p = _dot(jnp.concatenate(ws, axis=0).astype(BF16), bones_b)
        outs, off = [], 0
        for t in range(c):
            outs.append(jnp.sum(p[off:off + ns[t], :] * vc[0:ns[t], :], axis=0, keepdims=True))
            off += ns[t]
        oi_ref[sl, :] = jnp.concatenate(outs, axis=0)
        return carry
    lax.fori_loop(0, tm // c, chunk, 0, unroll=2)


def _hgrn_seq_kernel(oi_ref, qd_ref, kd_ref, v_ref, g_ref, ea_ref, s0_ref, ng_ref, o_ref, sfin_ref, st_sc,
                     *, bt, nsteps, dh, unroll):
    j = pl.program_id(1)
    w = qd_ref.shape[-1]
    heads = w // dh
    bones = _head_blocks(w, dh, F32)
    bones_b = bones.astype(BF16)

    @pl.when(j == 0)
    def _():
        def init(b, carry):
            s0 = s0_ref[b]
            st_sc[b] = (jnp.concatenate([s0] * heads, axis=1) * bones).T
            return carry
        lax.fori_loop(0, bt, init, 0)

    ng = ng_ref[...]

    def step(b, carry):
        st = st_sc[b]
        g = g_ref[b]
        o = oi_ref[b] + lax.dot_general(qd_ref[b].astype(BF16), st.astype(BF16), _NT, preferred_element_type=F32)
        upd = lax.dot_general(v_ref[b].astype(BF16), kd_ref[b].astype(BF16), _TN, preferred_element_type=F32)
        st_sc[b] = st * ea_ref[b][0:1, :] + upd * bones
        ms = _dot((o * o).astype(BF16), bones_b) * (1.0 / dh)
        o_ref[b] = o * lax.rsqrt(ms + EPS) * ng * (g * jax.nn.sigmoid(g))
        return carry
    lax.fori_loop(0, bt, step, 0, unroll=unroll)

    @pl.when(j == nsteps - 1)
    def _():
        def fin(b, carry):
            sbd = st_sc[b].T
            acc = sbd[:, 0:dh]
            for h in range(1, heads):
                acc = acc + sbd[:, h * dh:(h + 1) * dh]
            sfin_ref[b] = acc
            return carry
        lax.fori_loop(0, bt, fin, 0)


def _hgrn(q, lf, k, v, g, s0, norm_g, *, c, bt, dh):
    batch, t, w = q.shape
    rows = batch * t
    tm = math.gcd(rows, TM)
    flat = lambda a: a.reshape(rows, w)
    tile = pl.BlockSpec((tm, w), lambda i: (i, 0))
    oi, qd, kd, ea = pl.pallas_call(
        functools.partial(_hgrn_intra_kernel, c=c, dh=dh),
        grid=(rows // tm,),
        in_specs=[tile] * 4, out_specs=[tile] * 4,
        out_shape=[jax.ShapeDtypeStruct((rows, w), F32)] * 4,
        scratch_shapes=[pltpu.VMEM((tm, w), F32)],
        compiler_params=_cp("arbitrary"), name="hgrn_intra")(flat(q), flat(lf), flat(k), flat(v))
    nsteps = t // c
    tok = pl.BlockSpec((bt, c, w), lambda bi, j: (bi, j, 0))
    st = pl.BlockSpec((bt, w, dh), lambda bi, j: (bi, 0, 0))
    b3 = lambda a: a.reshape(batch, t, w)
    return pl.pallas_call(
        functools.partial(_hgrn_seq_kernel, bt=bt, nsteps=nsteps, dh=dh, unroll=4),
        grid=(batch // bt, nsteps),
        in_specs=[tok] * 6 + [st, _full((1, w))],
        out_specs=[tok, st],
        out_shape=[jax.ShapeDtypeStruct((batch, t, w), F32), jax.ShapeDtypeStruct((batch, w, dh), F32)],
        scratch_shapes=[pltpu.VMEM((bt, w, w), F32)],
        compiler_params=_cp("arbitrary", "arbitrary"), name="hgrn_seq")(
            b3(oi), b3(qd), b3(kd), v, g, b3(ea), s0, norm_g)


def _conv_kernel(glu_ref, buf_ref, w_ref, cb_ref, lg_ref, lb_ref, o_ref, nbuf_ref, ext_sc, *, bt, tt, taps, nt):
    t = pl.program_id(1)
    hist = CONV_HIST
    base = hist - (taps - 1)
    for bi in range(bt):
        @pl.when(t == 0)
        def _():
            ext_sc[bi, 0:hist, :] = buf_ref[bi]

        ext_sc[bi, hist:hist + tt, :] = glu_ref[bi]
        acc = w_ref[0:1, :] * ext_sc[bi, base:base + tt, :]
        for j in range(1, taps):
            acc = acc + w_ref[j:j + 1, :] * ext_sc[bi, base + j:base + j + tt, :]
        y = _ln(acc + cb_ref[...], lg_ref[...], lb_ref[...])
        o_ref[bi] = y * jax.nn.sigmoid(y)

        @pl.when(t == nt - 1)
        def _():
            nbuf_ref[bi] = ext_sc[bi, tt + base:tt + hist, :]

        ext_sc[bi, 0:hist, :] = ext_sc[bi, tt:tt + hist, :]


def _conv(glu, buf_padded, w, cb, lg, lb, *, tt, bt):
    batch, t, ch = glu.shape
    taps = w.shape[0]
    nt = t // tt
    return pl.pallas_call(
        functools.partial(_conv_kernel, bt=bt, tt=tt, taps=taps, nt=nt),
        grid=(batch // bt, nt),
        in_specs=[pl.BlockSpec((bt, tt, ch), lambda b, i: (b, i, 0)),
                  pl.BlockSpec((bt, CONV_HIST, ch), lambda b, i: (b, 0, 0)),
                  _full(w.shape), _full((1, ch)), _full((1, ch)), _full((1, ch))],
        out_specs=[pl.BlockSpec((bt, tt, ch), lambda b, i: (b, i, 0)),
                   pl.BlockSpec((bt, taps - 1, ch), lambda b, i: (b, 0, 0))],
        out_shape=[jax.ShapeDtypeStruct((batch, t, ch), F32), jax.ShapeDtypeStruct((batch, taps - 1, ch), F32)],
        scratch_shapes=[pltpu.VMEM((bt, CONV_HIST + tt, ch), F32)],
        compiler_params=_cp("arbitrary", "arbitrary"), name="conv")(glu, buf_padded, w, cb, lg, lb)


def _outproj_kernel(hp_ref, hs_ref, oa_ref, ob_ref, oc_ref, olat_ref, obs_ref, ocs_ref, wuv_ref, wout_ref, g_ref, b_ref,
                    wr_ref, br_ref, h1_ref, route_ref, routet_ref, mix_sc, *, ntp, alpha, heads, kv, dv, ne, ng):
    i = pl.program_id(0)
    da = heads * dv
    wb = ob_ref.shape[1]

    @pl.when(i < ntp)
    def _():
        mix_sc[...] = (alpha * hp_ref[...] + _dot(oa_ref[...], wout_ref[0:da, :])
                       + _dot(ob_ref[...].astype(BF16), wout_ref[da:da + wb, :])
                       + _dot(oc_ref[...].astype(BF16), wout_ref[da + wb:, :]))

    @pl.when(i >= ntp)
    def _():
        mix = (alpha * hs_ref[...] + _dot(obs_ref[...].astype(BF16), wout_ref[da:da + wb, :])
               + _dot(ocs_ref[...].astype(BF16), wout_ref[da + wb:, :]))
        for h in range(heads):
            oh = _dot(olat_ref[:, h * kv:(h + 1) * kv].astype(BF16), wuv_ref[h])
            mix = mix + _dot(oh.astype(BF16), wout_ref[h * dv:(h + 1) * dv, :])
        mix_sc[...] = mix

    half = mix_sc.shape[0] // 2
    for rs in (slice(0, half), slice(half, 2 * half)):
        _norm_and_route(mix_sc, h1_ref, route_ref, routet_ref, g_ref, b_ref, wr_ref, br_ref, rs, ne=ne, ng=ng)


def _norm_and_route(mix_sc, h1_ref, route_ref, routet_ref, g_ref, b_ref, wr_ref, br_ref, rs, *, ne, ng):
    h1 = _ln(mix_sc[rs, :], g_ref[...], b_ref[...])
    h1_ref[rs, :] = h1
    h_hi = h1.astype(BF16)
    h_lo = (h1 - h_hi.astype(F32)).astype(BF16)
    part = _dot(h_hi, wr_ref[...])
    logits = (part[:, 0:LANES] + part[:, LANES:] + _dot(h_lo, wr_ref[:, 0:LANES])
              + br_ref[...])
    tm = logits.shape[0]
    lane = lax.broadcasted_iota(I32, (tm, LANES), 1).astype(F32)
    big = jnp.float32(1e9)
    epg = ne // ng
    is_g = (lane >= ne) & (lane < ne + ng)
    gl = jnp.where(is_g, logits, -jnp.inf)
    gmax = jnp.max(gl, axis=1, keepdims=True)
    gidx = jnp.min(jnp.where(gl == gmax, lane, big), axis=1, keepdims=True) - ne
    p_g = 1.0 / jnp.sum(jnp.where(is_g, jnp.exp(logits - gmax), 0.0), axis=1, keepdims=True)
    lo = gidx * epg
    el = jnp.where((lane >= lo) & (lane < lo + epg), logits, -jnp.inf)
    v1 = jnp.max(el, axis=1, keepdims=True)
    e1 = jnp.min(jnp.where(el == v1, lane, big), axis=1, keepdims=True)
    el2 = jnp.where(lane == e1, -jnp.inf, el)
    v2 = jnp.max(el2, axis=1, keepdims=True)
    e2 = jnp.min(jnp.where(el2 == v2, lane, big), axis=1, keepdims=True)
    tt = jnp.exp(v2 - v1)
    w1 = p_g / (1.0 + tt)
    w2 = w1 * tt
    route = jnp.where(lane == 0, e1, jnp.where(lane == 1, e2, jnp.where(lane == 2, w1, jnp.where(lane == 3, w2, 0.0))))
    route_ref[rs, :] = route
    routet_ref[:, rs] = route.T[0:8, :]


def _outproj(h_p, h_s, oa_p, ob_p, oc_p, olat_s, ob_s, oc_s, lw, *, ntp, nts, alpha):
    d = h_p.shape[1]
    heads, kv, dv = lw["heads"], lw["kv"], lw["dv"]
    nt = ntp + nts
    t_all = nt * TM

    def pspec(n):
        return pl.BlockSpec((TM, n), lambda i: (jnp.minimum(i, ntp - 1), 0))

    def sspec(n):
        return pl.BlockSpec((TM, n), lambda i: (jnp.maximum(i - ntp, 0), 0))

    tok = pl.BlockSpec((TM, d), lambda i: (i, 0))
    return pl.pallas_call(
        functools.partial(_outproj_kernel, ntp=ntp, alpha=alpha, heads=heads, kv=kv, dv=dv, ne=lw["ne"], ng=lw["ng"]),
        grid=(nt,),
        in_specs=[pspec(d), sspec(d), pspec(oa_p.shape[1]), pspec(ob_p.shape[1]), pspec(oc_p.shape[1]),
                  sspec(olat_s.shape[1]), sspec(ob_s.shape[1]), sspec(oc_s.shape[1]),
                  _full(lw["w_uv_h"].shape), _full(lw["w_out"].shape), _full((1, d)), _full((1, d)),
                  _full(lw["w_r"].shape), _full((1, LANES))],
        out_specs=[tok, pl.BlockSpec((TM, LANES), lambda i: (i, 0)), pl.BlockSpec((8, TM), lambda i: (0, i))],
        out_shape=[jax.ShapeDtypeStruct((t_all, d), F32), jax.ShapeDtypeStruct((t_all, LANES), F32),
                   jax.ShapeDtypeStruct((8, t_all), F32)],
        scratch_shapes=[pltpu.VMEM((TM, d), F32)],
        compiler_params=_cp("arbitrary"), name="outproj")(
            h_p, h_s, oa_p, ob_p, oc_p, olat_s, ob_s, oc_s, lw["w_uv_h"], lw["w_out"], lw["ln1_g"], lw["ln1_b"],
            lw["w_r"], lw["b_r"])


def _plan_kernel(rt_ref, dest_ref, be_ref, meta_ref, seg_ref, cnt_sc, seg_sc, carry_sc, *, bm, ne, nbp):
    p = pl.program_id(0)
    first = pl.program_id(1) == 0
    tm = rt_ref.shape[1]
    e1, e2 = rt_ref[0:1, :], rt_ref[1:2, :]
    sub = lax.broadcasted_iota(I32, (LANES, tm), 0).astype(F32)
    oh1 = sub == e1
    oh2 = sub == e2
    ohs = jnp.where(oh1, 1.0, 0.0) + jnp.where(oh2, 1.0, 0.0)
    tile_cnt = jnp.sum(ohs, axis=1, keepdims=True)

    @pl.when((p == 0) & first)
    def _():
        cnt_sc[...] = jnp.zeros_like(cnt_sc)

    @pl.when(p == 0)
    def _():
        cnt_sc[...] = cnt_sc[...] + tile_cnt

    @pl.when((p == 1) & first)
    def _():
        cnt = cnt_sc[...]
        padded = jnp.ceil(cnt * (1.0 / bm)) * bm
        r = lax.broadcasted_iota(I32, (LANES, LANES), 0)
        cidx = lax.broadcasted_iota(I32, (LANES, LANES), 1)
        lstrict = jnp.where(cidx < r, 1.0, 0.0).astype(F32)
        seg_start = _dot_hi(lstrict, jnp.broadcast_to(padded, (LANES, LANES)))[:, 0:1]
        seg_sc[...] = seg_start
        carry_sc[...] = jnp.zeros_like(carry_sc)
        seg_end = seg_start + padded
        jl = lax.broadcasted_iota(I32, (LANES, nbp), 1).astype(F32) * bm
        be = jnp.sum(jnp.where(seg_end <= jl, 1.0, 0.0), axis=0, keepdims=True)
        be_ref[...] = jnp.minimum(be, ne - 1).astype(I32)
        total = jnp.max(seg_end, axis=0, keepdims=True)
        meta_ref[...] = jnp.broadcast_to(total * (1.0 / bm), (1, LANES)).astype(I32)
        eye = r == cidx
        seg_ref[0:1, :] = jnp.sum(jnp.where(eye, seg_start, 0.0), axis=0, keepdims=True).astype(I32)
        seg_ref[1:2, :] = jnp.sum(jnp.where(eye, seg_end, 0.0), axis=0, keepdims=True).astype(I32)

    @pl.when(p == 1)
    def _():
        rr = lax.broadcasted_iota(I32, (tm, tm), 0)
        ccol = lax.broadcasted_iota(I32, (tm, tm), 1)
        ustrict = jnp.where(rr < ccol, 1.0, 0.0).astype(BF16)
        base = _dot(ohs.astype(BF16), ustrict) + carry_sc[...] + seg_sc[...]
        d1 = jnp.sum(jnp.where(oh1, base, 0.0), axis=0, keepdims=True)
        d2 = jnp.sum(jnp.where(oh2, base, 0.0), axis=0, keepdims=True)
        dest_ref[0:1, :] = d1.astype(I32)
        dest_ref[1:2, :] = d2.astype(I32)
        carry_sc[...] = carry_sc[...] + tile_cnt


def _plan(route_t, *, ne, nb_max):
    t_all = route_t.shape[1]
    tm = 512 if t_all % 512 == 0 else TM
    nt = t_all // tm
    nbp = pl.cdiv(nb_max, LANES) * LANES
    return pl.pallas_call(
        functools.partial(_plan_kernel, bm=MOE_BM, ne=ne, nbp=nbp),
        grid=(2, nt),
        in_specs=[pl.BlockSpec((8, tm), lambda p, i: (0, i))],
        out_specs=[pl.BlockSpec((2, tm), lambda p, i: (0, i * p)), _full((1, nbp)), _full((1, LANES)),
                   _full((2, LANES))],
        out_shape=[jax.ShapeDtypeStruct((2, t_all), I32), jax.ShapeDtypeStruct((1, nbp), I32),
                   jax.ShapeDtypeStruct((1, LANES), I32), jax.ShapeDtypeStruct((2, LANES), I32)],
        scratch_shapes=[pltpu.VMEM((LANES, 1), F32)] * 3,
        compiler_params=_cp("arbitrary", "arbitrary"), name="moe_plan")(route_t)


def _scatter_kernel(d1_ref, d2_ref, seg_ref, meta_ref, h_ref, rows_hbm, zbuf, sem, zsem, *, tm, ne, bm, nb):
    @pl.when(pl.program_id(0) == 0)
    def _():
        zbuf[...] = jnp.zeros_like(zbuf)
        for e in range(ne):
            @pl.when(seg_ref[1, e] > seg_ref[0, e])
            def _():
                last = pl.multiple_of(seg_ref[1, e] - bm, bm)
                pltpu.make_async_copy(zbuf, rows_hbm.at[pl.ds(last, bm)], zsem).start()

        def ztail(jb, carry):
            pltpu.make_async_copy(zbuf, rows_hbm.at[pl.ds(pl.multiple_of(jb * bm, bm), bm)], zsem).start()
            return carry
        lax.fori_loop(meta_ref[0, 0], nb, ztail, 0)
        for e in range(ne):
            @pl.when(seg_ref[1, e] > seg_ref[0, e])
            def _():
                pltpu.make_async_copy(zbuf, rows_hbm.at[pl.ds(0, bm)], zsem).wait()

        def zwait(jb, carry):
            pltpu.make_async_copy(zbuf, rows_hbm.at[pl.ds(0, bm)], zsem).wait()
            return carry
        lax.fori_loop(meta_ref[0, 0], nb, zwait, 0)

    for r in range(tm):
        src = h_ref.at[pl.ds(r, 1)]
        pltpu.make_async_copy(src, rows_hbm.at[pl.ds(d1_ref[0, 0, r], 1)], sem.at[0]).start(priority=0)
        pltpu.make_async_copy(src, rows_hbm.at[pl.ds(d2_ref[0, 0, r], 1)], sem.at[1]).start(priority=1)
    pltpu.make_async_copy(h_ref, rows_hbm.at[pl.ds(0, tm)], sem.at[0]).wait()
    pltpu.make_async_copy(h_ref, rows_hbm.at[pl.ds(0, tm)], sem.at[1]).wait()


def _scatter(h1, d1, d2, seg, meta, nrows, *, ne):
    t_all, d = h1.shape
    tm = d1.shape[2]
    dspec = pl.BlockSpec((1, 1, tm), lambda i: (i, 0, 0), memory_space=pltpu.SMEM)
    smem = pl.BlockSpec(memory_space=pltpu.SMEM)
    return pl.pallas_call(
        functools.partial(_scatter_kernel, tm=tm, ne=ne, bm=MOE_BM, nb=nrows // MOE_BM),
        grid=(t_all // tm,),
        in_specs=[dspec, dspec, smem, smem, pl.BlockSpec((tm, d), lambda i: (i, 0))],
        out_specs=pl.BlockSpec(memory_space=pl.ANY),
        out_shape=jax.ShapeDtypeStruct((nrows, d), F32),
        scratch_shapes=[pltpu.VMEM((MOE_BM, d), F32), pltpu.SemaphoreType.DMA((2,)), pltpu.SemaphoreType.DMA(())],
        compiler_params=_cp("arbitrary"), name="moe_scatter")(d1, d2, seg, meta, h1)


def _ffn_kernel(be_ref, na_ref, x_ref, wg_ref, wu_ref, wd_ref, y_ref, wgb, wub, wdb):
    i = pl.program_id(0)

    @pl.when(i < na_ref[0])
    def _():
        @pl.when((i == 0) | (be_ref[i] != be_ref[jnp.maximum(i - 1, 0)]))
        def _():
            wgb[...] = wg_ref[...].astype(BF16)
            wub[...] = wu_ref[...].astype(BF16)
            wdb[...] = wd_ref[...].astype(BF16)

        half = x_ref.shape[0] // 2
        for rs in (slice(0, half), slice(half, 2 * half)):
            x = x_ref[rs, :].astype(BF16)
            g = _dot(x, wgb[...])
            u = _dot(x, wub[...])
            hdn = (g * jax.nn.sigmoid(g) * u).astype(BF16)
            y_ref[rs, :] = _dot(hdn, wdb[...])

    @pl.when(i >= na_ref[0])
    def _():
        y_ref[...] = jnp.zeros_like(y_ref)


def _ffn(rows, be, nact, w_gate, w_up, w_down, layer):
    nrows, d = rows.shape
    ff = w_gate.shape[3]
    nb = nrows // MOE_BM

    def blk(i, be_r, na_r):
        return jnp.minimum(i, na_r[0] - 1)

    def wspec(a, b):
        return pl.BlockSpec((None, None, a, b), lambda i, be_r, na_r: (layer, be_r[blk(i, be_r, na_r)], 0, 0))

    row_spec = pl.BlockSpec((MOE_BM, d), lambda i, be_r, na_r: (blk(i, be_r, na_r), 0))
    return pl.pallas_call(
        _ffn_kernel,
        grid_spec=pltpu.PrefetchScalarGridSpec(
            num_scalar_prefetch=2, grid=(nb,),
            in_specs=[row_spec, wspec(d, ff), wspec(d, ff), wspec(ff, d)],
            out_specs=pl.BlockSpec((MOE_BM, d), lambda i, be_r, na_r: (i, 0)),
            scratch_shapes=[pltpu.VMEM((d, ff), BF16), pltpu.VMEM((d, ff), BF16), pltpu.VMEM((ff, d), BF16)]),
        out_shape=jax.ShapeDtypeStruct((nrows, d), F32),
        compiler_params=_cp("arbitrary"), name="moe_ffn")(be, nact, rows, w_gate, w_up, w_down)


def _combine_kernel(d1_ref, d2_ref, d1n_ref, d2n_ref, h1_ref, route_ref, yb_hbm, g_ref, b_ref, outp_ref, outs_ref,
                    buf1, buf2, sem, *, tm, nt, ntp, alpha):
    i = pl.program_id(0)
    slot = lax.rem(i, 2)

    def gather(da_ref, db_ref, s):
        def start(r, carry):
            pltpu.make_async_copy(yb_hbm.at[pl.ds(da_ref[0, 0, r], 1)], buf1.at[s, pl.ds(r, 1)],
                                  sem.at[0, s]).start(priority=0)
            pltpu.make_async_copy(yb_hbm.at[pl.ds(db_ref[0, 0, r], 1)], buf2.at[s, pl.ds(r, 1)],
                                  sem.at[1, s]).start(priority=1)
            return carry
        lax.fori_loop(0, tm, start, 0, unroll=8)

    @pl.when(i == 0)
    def _():
        gather(d1_ref, d2_ref, 0)

    @pl.when(i + 1 < nt)
    def _():
        gather(d1n_ref, d2n_ref, 1 - slot)

    pltpu.make_async_copy(yb_hbm.at[pl.ds(0, tm)], buf1.at[slot], sem.at[0, slot]).wait()
    pltpu.make_async_copy(yb_hbm.at[pl.ds(0, tm)], buf2.at[slot], sem.at[1, slot]).wait()
    moe = route_ref[:, 2:3] * buf1[slot] + route_ref[:, 3:4] * buf2[slot]
    out = _ln(alpha * h1_ref[...] + moe, g_ref[...], b_ref[...])

    @pl.when(i < ntp)
    def _():
        outp_ref[...] = out

    @pl.when(i >= ntp)
    def _():
        outs_ref[...] = out


def _combine(h1, route, yb, d1, d2, g, b, *, ntp, alpha):
    t_all, d = h1.shape
    tm = d1.shape[2]
    nt = t_all // tm
    dspec = pl.BlockSpec((1, 1, tm), lambda i: (i, 0, 0), memory_space=pltpu.SMEM)
    dnext = pl.BlockSpec((1, 1, tm), lambda i: (jnp.minimum(i + 1, nt - 1), 0, 0), memory_space=pltpu.SMEM)
    tok = pl.BlockSpec((tm, d), lambda i: (i, 0))
    return pl.pallas_call(
        functools.partial(_combine_kernel, tm=tm, nt=nt, ntp=ntp, alpha=alpha),
        grid=(nt,),
        in_specs=[dspec, dspec, dnext, dnext, tok, pl.BlockSpec((tm, LANES), lambda i: (i, 0)),
                  pl.BlockSpec(memory_space=pl.ANY), _full((1, d)), _full((1, d))],
        out_specs=[pl.BlockSpec((tm, d), lambda i: (jnp.minimum(i, ntp - 1), 0)),
                   pl.BlockSpec((tm, d), lambda i: (jnp.maximum(i - ntp, 0), 0))],
        out_shape=[jax.ShapeDtypeStruct((ntp * tm, d), F32), jax.ShapeDtypeStruct(((nt - ntp) * tm, d), F32)],
        scratch_shapes=[pltpu.VMEM((2, tm, d), F32), pltpu.VMEM((2, tm, d), F32), pltpu.SemaphoreType.DMA((2, 2))],
        compiler_params=_cp("arbitrary"), name="moe_combine")(d1, d2, d1, d2, h1, route, yb, g, b)


def _rope_tables(pos, dr, scale):
    half = dr // 2
    inv = ROPE_BASE ** (-jnp.arange(half, dtype=F32) / half)
    ang = pos.astype(F32)[:, None] * inv[None, :]
    cos, sin = jnp.cos(ang), jnp.sin(ang)
    n = pos.shape[0]
    z64, z32, o64 = jnp.zeros((n, 64), F32), jnp.zeros((n, 32), F32), jnp.ones((n, 64), F32)
    cq = jnp.concatenate([o64, cos, cos, z32], axis=1) * scale
    sq = jnp.concatenate([z64, -sin, sin, z32], axis=1) * scale
    ck = jnp.concatenate([z64, cos, cos, z32], axis=1)
    sk = jnp.concatenate([z64, -sin, sin, z32], axis=1)
    return cq, sq, ck, sk


def _prep_layer(l, p):
    w_in = p["w_in"][l]
    d = w_in.shape[0]
    ql, kv = p["mla_q_norm"].shape[1], p["mla_kv_norm"].shape[1]
    heads, dn = p["mla_w_uk"].shape[2], p["mla_w_uk"].shape[3]
    dv = p["mla_w_uv"].shape[3]
    dr = p["mla_w_uq"].shape[2] // heads - dn
    assert dn == 64 and dr == 32 and dv == 64 and ql % LANES == 0 and kv % LANES == 0
    hw = p["hgrn_lb_logits"].shape[1]
    ch = p["conv_b"].shape[1]
    assert hw == 256 and ch == 256
    o = ql + kv
    kr = w_in[:, o:o + dr]
    x1, x2 = kr[:, :dr // 2], kr[:, dr // 2:]
    seg = jnp.concatenate([jnp.zeros((d, 64), F32), x1, x2, x2, x1], axis=1)
    w_in_p = jnp.concatenate([w_in[:, :o], seg, w_in[:, o + dr:]], axis=1).astype(BF16)
    uq = p["mla_w_uq"][l].reshape(ql, heads, dn + dr)
    u1, u2 = uq[:, :, dn:dn + dr // 2], uq[:, :, dn + dr // 2:]
    w_uq = jnp.concatenate([uq[:, :, :dn], u1, u2, u2, u1], axis=2).reshape(ql, heads * LANES).astype(BF16)
    uk = p["mla_w_uk"][l]
    w_uk = jnp.concatenate([uk, jnp.zeros_like(uk)], axis=2).reshape(kv, heads * LANES).astype(BF16)
    w_ukt = jnp.transpose(uk, (1, 2, 0)).astype(BF16)
    uv = p["mla_w_uv"][l]
    lb = p["lb_all"][l]
    lbc = jnp.zeros((8, hw), F32).at[0].set(jnp.log(lb)).at[1].set(jnp.log1p(-lb)).at[2].set(1.0 - lb)
    ne, ng = p["router_e_w"].shape[2], p["router_g_w"].shape[2]
    w_r = jnp.concatenate([p["router_e_w"][l], p["router_g_w"][l], jnp.zeros((d, LANES - ne - ng), F32)], axis=1)
    b_r = jnp.concatenate([p["router_e_b"][l], p["router_g_b"][l], jnp.zeros((LANES - ne - ng,), F32)])[None, :]
    w_r_hi = w_r.astype(BF16)
    w_r = jnp.concatenate([w_r_hi, (w_r - w_r_hi.astype(F32)).astype(BF16)], axis=1)
    return dict(
        ql=ql, kv=kv, heads=heads, dn=dn, dv=dv, dr=dr, ne=ne, ng=ng,
        w_in=w_in_p, q_norm=p["mla_q_norm"][l][None, :], w_uq=w_uq, kv_norm=p["mla_kv_norm"][l][None, :],
        w_uk=w_uk, w_ukt=w_ukt, w_uv=uv.reshape(kv, heads * dv).astype(BF16),
        w_uv_h=jnp.transpose(uv, (1, 0, 2)).astype(BF16), lbc=lbc,
        hgrn_norm=jnp.tile(p["hgrn_norm"][l], hw // p["hgrn_norm"].shape[1])[None, :],
        conv_w=p["conv_w"][l], conv_b=p["conv_b"][l][None, :],
        conv_ln_g=p["conv_ln_g"][l][None, :], conv_ln_b=p["conv_ln_b"][l][None, :],
        w_out=p["w_out"][l].astype(BF16), ln1_g=p["ln1_g"][l][None, :], ln1_b=p["ln1_b"][l][None, :],
        ln2_g=p["ln2_g"][l][None, :], ln2_b=p["ln2_b"][l][None, :], w_r=w_r, b_r=b_r)


def kernel(x_prompt, x_sample, cache_ckv, cache_krope, state_hgrn, state_conv, page_table, w_in, mla_q_norm, mla_w_uq, mla_kv_norm, mla_w_uk, mla_w_uv, hgrn_lb_logits, hgrn_norm, conv_w, conv_b, conv_ln_g, conv_ln_b, w_out, ln1_g, ln1_b, ln2_g, ln2_b, router_g_w, router_g_b, router_e_w, router_e_b, exp_w_gate, exp_w_up, exp_w_down):
    batch, seq, d = x_prompt.shape
    nseq, tnew, _ = x_sample.shape
    depth = w_in.shape[0]
    t_p, t_s = batch * seq, nseq * tnew
    t_all = t_p + t_s
    assert t_p % TM == 0 and t_s % TM == 0 and seq % TM == 0 and seq % HGRN_C == 0
    ntp, nts = t_p // TM, t_s // TM
    past_len = page_table.shape[1] * cache_ckv.shape[2]
    alpha = (2 * depth) ** 0.25
    hb, dkb = state_hgrn.shape[2], state_hgrn.shape[3]
    taps = conv_w.shape[1]
    assert taps - 1 <= CONV_HIST

    lbp = jax.nn.softmax(hgrn_lb_logits.astype(F32), axis=0)
    lbcs = jnp.cumsum(lbp, axis=0)
    params = dict(w_in=w_in, mla_q_norm=mla_q_norm, mla_w_uq=mla_w_uq, mla_kv_norm=mla_kv_norm, mla_w_uk=mla_w_uk,
                  mla_w_uv=mla_w_uv, hgrn_lb_logits=hgrn_lb_logits, lb_all=lbcs - lbcs[0:1], hgrn_norm=hgrn_norm,
                  conv_w=conv_w, conv_b=conv_b, conv_ln_g=conv_ln_g, conv_ln_b=conv_ln_b, w_out=w_out,
                  ln1_g=ln1_g, ln1_b=ln1_b, ln2_g=ln2_g, ln2_b=ln2_b, router_g_w=router_g_w, router_g_b=router_g_b,
                  router_e_w=router_e_w, router_e_b=router_e_b)
    dn = mla_w_uk.shape[3]
    dr = cache_krope.shape[3]
    scale = (dn + dr) ** -0.5
    tabs_p = _rope_tables(jnp.arange(seq), dr, scale)
    tabs_s = _rope_tables(jnp.tile(past_len + jnp.arange(tnew), nseq), dr, scale)

    ne = router_e_w.shape[2]
    n_assign = 2 * t_all
    nb_max = (n_assign + ne * (MOE_BM - 1) + MOE_BM - 1) // MOE_BM
    nrows = nb_max * MOE_BM

    h_p, h_s = x_prompt.reshape(t_p, d), x_sample.reshape(t_s, d)
    outs = {k: [] for k in ("ckv_p", "kr_p", "sh_p", "sc_p", "ckv_s", "kr_s", "sh_s", "sc_s")}
    hw = hb * dkb
    tpad = 8
    bt_s = math.gcd(nseq, 16)
    for l in range(depth):
        lw = _prep_layer(l, params)
        heads, kv, dv = lw["heads"], lw["kv"], lw["dv"]
        (qatt, katt, vatt, ckv_p, kr_p, hq, hlf, hk, hv, hg, glu) = _inproj(
            h_p, lw, tabs_p, tab_period=seq // TM, sample=False)
        oa_p = _attn_prompt(qatt, katt, vatt, batch=batch, seq=seq, heads=heads, dv=dv)
        r3 = lambda a: a.reshape(batch, seq, a.shape[1])
        ob_p, s_p = _hgrn(r3(hq), r3(hlf), r3(hk), r3(hv), r3(hg), jnp.zeros((batch, hw, dkb), F32), lw["hgrn_norm"],
                          c=HGRN_C, bt=batch, dh=dkb)
        oc_p, buf_p = _conv(r3(glu), jnp.zeros((batch, CONV_HIST, glu.shape[1]), F32), lw["conv_w"], lw["conv_b"],
                            lw["conv_ln_g"], lw["conv_ln_b"], tt=TM, bt=1)
        (qatt_s, qlat_s, ckv_s, kr_s, hq, hlf, hk, hv, hg, glu_s) = _inproj(
            h_s, lw, tabs_s, tab_period=nts, sample=True)
        olat_s = _attn_sample(page_table, qlat_s, qatt_s, ckv_s, kr_s, cache_ckv, cache_krope, l,
                              heads=heads, tnew=tnew, dn=lw["dn"])
        r3s = lambda a: jnp.pad(a.reshape(nseq, tnew, a.shape[1]), ((0, 0), (0, tpad - tnew), (0, 0)))
        ob_s, s_s = _hgrn(r3s(hq), r3s(hlf), r3s(hk), r3s(hv), r3s(hg), state_hgrn[l].reshape(nseq, hw, dkb),
                          lw["hgrn_norm"], c=tpad, bt=bt_s, dh=dkb)
        ob_s = ob_s[:, :tnew].reshape(t_s, hw)
        bufpad = jnp.pad(state_conv[l], ((0, 0), (CONV_HIST - (taps - 1), 0), (0, 0)))
        oc_s, buf_s = _conv(glu_s.reshape(nseq, tnew, -1), bufpad, lw["conv_w"], lw["conv_b"],
                            lw["conv_ln_g"], lw["conv_ln_b"], tt=tnew, bt=math.gcd(nseq, 8))
        h1, route, route_t = _outproj(h_p, h_s, oa_p, ob_p.reshape(t_p, hw), oc_p.reshape(t_p, -1), olat_s, ob_s,
                                      oc_s.reshape(t_s, -1), lw, ntp=ntp, nts=nts, alpha=alpha)
        dest, be, meta, seg = _plan(route_t, ne=ne, nb_max=nb_max)
        d1 = dest[0].reshape(t_all // TM, 1, TM)
        d2 = dest[1].reshape(t_all // TM, 1, TM)
        rows = _scatter(h1, d1, d2, seg, meta, nrows, ne=ne)
        yb = _ffn(rows, be.reshape(-1), meta[0, 0:1], exp_w_gate, exp_w_up, exp_w_down, l)
        h_p, h_s = _combine(h1, route, yb, d1, d2, lw["ln2_g"], lw["ln2_b"], ntp=ntp, alpha=alpha)

        outs["ckv_p"].append(ckv_p.reshape(batch, seq, kv))
        outs["kr_p"].append(kr_p.reshape(batch, seq, dr))
        outs["sh_p"].append(s_p.reshape(batch, hb, dkb, dkb))
        outs["sc_p"].append(buf_p)
        outs["ckv_s"].append(ckv_s.reshape(nseq, tnew, kv))
        outs["kr_s"].append(kr_s.reshape(nseq, tnew, dr))
        outs["sh_s"].append(s_s.reshape(nseq, hb, dkb, dkb))
        outs["sc_s"].append(buf_s)
    st = lambda k: jnp.stack(outs[k])
    return (h_p.reshape(batch, seq, d), h_s.reshape(nseq, tnew, d),
            st("ckv_p"), st("kr_p"), st("sh_p"), st("sc_p"), st("ckv_s"), st("kr_s"), st("sh_s"), st("sc_s"))
```

```python
import functools
import math

import jax
import jax.numpy as jnp
from jax import lax
from jax.experimental import pallas as pl
from jax.experimental.pallas import tpu as pltpu

F32 = jnp.float32
BF16 = jnp.bfloat16
I32 = jnp.int32

LANES = 128
ROPE_BASE = 10000.0
EPS = 1e-6
TM = 256
MOE_BM = 512
HGRN_C = 16
CONV_HIST = 32
SAMPLE_PAGES = 32
NSLOT = 3
VMEM_LIMIT = 48 * 1024 * 1024
_HI = lax.Precision.HIGHEST
_NT = (((1,), (1,)), ((), ()))
_TN = (((0,), (0,)), ((), ()))


def _cp(*sem):
    return pltpu.CompilerParams(dimension_semantics=sem, vmem_limit_bytes=VMEM_LIMIT)


def _dot(a, b):
    return jnp.dot(a, b, preferred_element_type=F32)


def _dot_hi(a, b):
    return jnp.dot(a, b, preferred_element_type=F32, precision=_HI)


def _rms(x, g):
    return x * lax.rsqrt(jnp.mean(x * x, axis=-1, keepdims=True) + EPS) * g


def _ln(x, g, b):
    mu = jnp.mean(x, axis=-1, keepdims=True)
    xc = x - mu
    var = jnp.mean(xc * xc, axis=-1, keepdims=True)
    return xc * lax.rsqrt(var + EPS) * g + b


def _full(shape):
    n = len(shape)
    return pl.BlockSpec(shape, lambda *_: (0,) * n)


def _proj_common(x, win_ref, qn_ref, wuq_ref, kvn_ref, cq_ref, sq_ref, ck_ref, sk_ref, lb_ref, *, ql, kv, heads):
    o_kr = ql + kv
    o_h = o_kr + LANES
    o_c = o_h + 4 * 256
    u = _dot(x, win_ref[...])
    y = _rms(u[:, 0:ql], qn_ref[...]).astype(BF16)
    q = _dot(y, wuq_ref[...])
    cqt, sqt = cq_ref[...], sq_ref[...]
    qg = []
    for g in range(heads):
        t = q[:, g * LANES:(g + 1) * LANES]
        qg.append(t * cqt + pltpu.roll(t, 96, axis=1) * sqt)
    ckv = _rms(u[:, ql:o_kr], kvn_ref[...])
    kr = u[:, o_kr:o_h]
    kr_rot = kr * ck_ref[...] + pltpu.roll(kr, 96, axis=1) * sk_ref[...]
    uh = u[:, o_h:o_c]
    hq, z, hv, hg = uh[:, 0:256], uh[:, 256:512], uh[:, 512:768], uh[:, 768:1024]
    log_lb, log1m_lb, one_m_lb = lb_ref[0:1, :], lb_ref[1:2, :], lb_ref[2:3, :]
    log_sig = jnp.minimum(z, 0.0) - jnp.log1p(jnp.exp(-jnp.abs(z)))
    b = log1m_lb + log_sig
    hlf = jnp.maximum(log_lb, b) + jnp.log1p(jnp.exp(-jnp.abs(log_lb - b)))
    hk = one_m_lb * jax.nn.sigmoid(-z)
    uc = u[:, o_c:o_c + 512]
    glu = uc[:, 0:256] * jax.nn.sigmoid(uc[:, 256:512])
    return qg, ckv, kr_rot, (hq, hlf, hk, hv, hg), glu


def _inproj_prompt_kernel(h_ref, win_ref, qn_ref, wuq_ref, kvn_ref, cq_ref, sq_ref, ck_ref, sk_ref, lb_ref,
                          wuk_ref, wuv_ref,
                          qatt_ref, katt_ref, vatt_ref, ckv_ref, kr_ref, hq_ref, hlf_ref, hk_ref, hv_ref, hg_ref,
                          glu_ref, *, ql, kv, heads):
    x = h_ref[...].astype(BF16)
    qg, ckv, kr_rot, hg5, glu = _proj_common(x, win_ref, qn_ref, wuq_ref, kvn_ref, cq_ref, sq_ref, ck_ref, sk_ref,
                                             lb_ref, ql=ql, kv=kv, heads=heads)
    cb = ckv.astype(BF16)
    kn = _dot(cb, wuk_ref[...])
    for g in range(heads):
        sl = slice(g * LANES, (g + 1) * LANES)
        qatt_ref[:, sl] = qg[g].astype(BF16)
        katt_ref[:, sl] = (kn[:, sl] + kr_rot).astype(BF16)
    vatt_ref[...] = _dot(cb, wuv_ref[...]).astype(BF16)
    ckv_ref[...] = ckv
    kr_ref[...] = pltpu.roll(kr_rot, 64, axis=1)[:, 0:32]
    for r, v in zip((hq_ref, hlf_ref, hk_ref, hv_ref, hg_ref), hg5):
        r[...] = v
    glu_ref[...] = glu


def _inproj_sample_kernel(h_ref, win_ref, qn_ref, wuq_ref, kvn_ref, cq_ref, sq_ref, ck_ref, sk_ref, lb_ref,
                          wukt_ref,
                          qatt_ref, qlat_ref, ckv_ref, kr_ref, hq_ref, hlf_ref, hk_ref, hv_ref, hg_ref,
                          glu_ref, *, ql, kv, heads, dn):
    x = h_ref[...].astype(BF16)
    qg, ckv, kr_rot, hg5, glu = _proj_common(x, win_ref, qn_ref, wuq_ref, kvn_ref, cq_ref, sq_ref, ck_ref, sk_ref,
                                             lb_ref, ql=ql, kv=kv, heads=heads)
    for g in range(heads):
        qatt_ref[:, g * LANES:(g + 1) * LANES] = qg[g].astype(BF16)
        qn = qg[g][:, 0:dn].astype(BF16)
        qlat_ref[:, g * kv:(g + 1) * kv] = _dot(qn, wukt_ref[g]).astype(BF16)
    ckv_ref[...] = ckv
    kr_ref[...] = pltpu.roll(kr_rot, 64, axis=1)[:, 0:32]
    for r, v in zip((hq_ref, hlf_ref, hk_ref, hv_ref, hg_ref), hg5):
        r[...] = v
    glu_ref[...] = glu


def _inproj(h, lw, tabs, *, tab_period, sample):
    t, d = h.shape
    ntiles = t // TM
    ql, kv, heads = lw["ql"], lw["kv"], lw["heads"]
    tab_spec = pl.BlockSpec((TM, LANES), lambda i: (i % tab_period, 0))
    in_specs = [pl.BlockSpec((TM, d), lambda i: (i, 0)),
                _full(lw["w_in"].shape), _full((1, ql)), _full(lw["w_uq"].shape), _full((1, kv)),
                tab_spec, tab_spec, tab_spec, tab_spec, _full((8, 256))]
    args = [h, lw["w_in"], lw["q_norm"], lw["w_uq"], lw["kv_norm"], *tabs, lw["lbc"]]

    def tok(n, dt):
        return jax.ShapeDtypeStruct((t, n), dt), pl.BlockSpec((TM, n), lambda i: (i, 0))

    tail = [tok(kv, F32), tok(32, F32)] + [tok(256, F32)] * 6
    if sample:
        in_specs += [_full(lw["w_ukt"].shape)]
        args += [lw["w_ukt"]]
        outs = [tok(heads * LANES, BF16), tok(heads * kv, BF16)] + tail
        body = functools.partial(_inproj_sample_kernel, ql=ql, kv=kv, heads=heads, dn=lw["dn"])
        name = "inproj_sample"
    else:
        in_specs += [_full(lw["w_uk"].shape), _full(lw["w_uv"].shape)]
        args += [lw["w_uk"], lw["w_uv"]]
        outs = [tok(heads * LANES, BF16), tok(heads * LANES, BF16), tok(lw["w_uv"].shape[1], BF16)] + tail
        body = functools.partial(_inproj_prompt_kernel, ql=ql, kv=kv, heads=heads)
        name = "inproj_prompt"
    return pl.pallas_call(
        body, grid=(ntiles,), in_specs=in_specs,
        out_specs=[o[1] for o in outs], out_shape=[o[0] for o in outs],
        compiler_params=_cp("arbitrary"), name=name)(*args)


def _attn_prompt_kernel(q_ref, k_ref, v_ref, o_ref, *, seq, tq, dv):
    key = lax.broadcasted_iota(I32, (tq, tq), 0)
    qry = lax.broadcasted_iota(I32, (tq, tq), 1)
    causal = key <= qry
    vall = v_ref[...]
    vh = [vall[:, hh * dv:(hh + 1) * dv] for hh in range(2)]
    for qi in range(seq // tq):
        q0, q1 = qi * tq, (qi + 1) * tq
        outs = []
        for hh in range(2):
            hs = slice(hh * LANES, (hh + 1) * LANES)
            q = q_ref[q0:q1, hs]
            sd = lax.dot_general(k_ref[q0:q1, hs], q, _NT, preferred_element_type=F32)
            sd = jnp.where(causal, sd, -jnp.inf)
            m = jnp.max(sd, axis=0, keepdims=True)
            if qi > 0:
                so = lax.dot_general(k_ref[0:q0, hs], q, _NT, preferred_element_type=F32)
                m = jnp.maximum(m, jnp.max(so, axis=0, keepdims=True))
                po = jnp.exp2(so - m)
            pd = jnp.exp2(sd - m)
            l = jnp.sum(pd, axis=0, keepdims=True)
            o = lax.dot_general(vh[hh][q0:q1, :], pd.astype(BF16), _TN, preferred_element_type=F32)
            if qi > 0:
                l = l + jnp.sum(po, axis=0, keepdims=True)
                o = o + lax.dot_general(vh[hh][0:q0, :], po.astype(BF16), _TN, preferred_element_type=F32)
            outs.append(o / l)
        o_ref[q0:q1, :] = jnp.concatenate(outs, axis=0).T.astype(o_ref.dtype)


def _attn_prompt(qatt, katt, vatt, *, batch, seq, heads, dv):
    assert heads % 2 == 0 and 2 * dv == LANES
    tq = min(512, seq)
    q3 = qatt.reshape(batch, seq, heads * LANES)
    k3 = katt.reshape(batch, seq, heads * LANES)
    v3 = vatt.reshape(batch, seq, heads * dv)
    out = pl.pallas_call(
        functools.partial(_attn_prompt_kernel, seq=seq, tq=tq, dv=dv),
        grid=(batch, heads // 2),
        in_specs=[pl.BlockSpec((None, seq, 2 * LANES), lambda b, h: (b, 0, h)),
                  pl.BlockSpec((None, seq, 2 * LANES), lambda b, h: (b, 0, h)),
                  pl.BlockSpec((None, seq, LANES), lambda b, h: (b, 0, h))],
        out_specs=pl.BlockSpec((None, seq, LANES), lambda b, h: (b, 0, h)),
        out_shape=jax.ShapeDtypeStruct((batch, seq, heads * dv), BF16),
        compiler_params=_cp("arbitrary", "arbitrary"), name="attn_prompt")(q3, k3, v3)
    return out.reshape(batch * seq, heads * dv)


def _attn_sample_kernel(pt_ref, qlat_ref, qatt_ref, cnew_ref, knew_ref, ckv_hbm, krt_hbm, o_ref, kbuf, rbuf, sem,
                        *, layer, pages, psize, nchunks, nseq, heads, tnew, dn, dr):
    b = pl.program_id(0)

    def fetch(seq, chunk, slot):
        for i in range(pages):
            page = pt_ref[seq, chunk * pages + i]
            keys = pl.ds(i * psize, psize)
            pltpu.make_async_copy(ckv_hbm.at[layer, page], kbuf.at[slot, keys], sem.at[0, slot]).start()
            pltpu.make_async_copy(krt_hbm.at[layer, page], rbuf.at[slot, :, keys], sem.at[1, slot]).start()

    unguarded = nchunks >= NSLOT - 1

    def fetch_ahead(g_next):
        step_ahead, chunk = divmod(g_next, nchunks)
        slot = lax.rem(b * nchunks + g_next, NSLOT)
        if step_ahead == 0:
            fetch(b, chunk, slot)
        elif unguarded:
            fetch(jnp.minimum(b + step_ahead, nseq - 1), chunk, slot)
        else:
            pl.when(b + step_ahead < nseq)(lambda: fetch(b + step_ahead, chunk, slot))

    @pl.when(b == 0)
    def _():
        for g0 in range(NSLOT - 1):
            fetch_ahead(g0)

    ql = qlat_ref[...]
    qr = qatt_ref[:, dn:dn + dr]
    rows, kv = ql.shape

    def wait(slot):
        pltpu.make_async_copy(kbuf.at[slot], kbuf.at[slot], sem.at[0, slot]).wait()
        pltpu.make_async_copy(rbuf.at[slot], rbuf.at[slot], sem.at[1, slot]).wait()

    def scores(slot):
        kc = kbuf[slot].astype(BF16)
        krt = rbuf[slot].astype(BF16)
        return lax.dot_general(ql, kc, _NT, preferred_element_type=F32) + _dot(qr, krt)

    slot0 = lax.rem(b * nchunks, NSLOT)
    wait(slot0)
    s_cur = scores(slot0)
    m1 = jnp.full((rows, 1), -jnp.inf, F32)
    l1 = jnp.zeros((rows, 1), F32)
    acc1 = jnp.zeros((rows, kv), F32)
    for c in range(nchunks):
        slot = lax.rem(b * nchunks + c, NSLOT)
        fetch_ahead(c + NSLOT - 1)
        if c + 1 < nchunks:
            slot_n = lax.rem(b * nchunks + c + 1, NSLOT)
            wait(slot_n)
            s_next = scores(slot_n)
        m_new = jnp.maximum(m1, jnp.max(s_cur, axis=1, keepdims=True))
        alpha = jnp.exp2(m1 - m_new)
        p = jnp.exp2(s_cur - m_new)
        l1 = alpha * l1 + jnp.sum(p, axis=1, keepdims=True)
        acc1 = alpha * acc1 + _dot(p.astype(BF16), kbuf[slot].astype(BF16))
        m1 = m_new
        if c + 1 < nchunks:
            s_cur = s_next

    if unguarded:
        @pl.when(b == nseq - 1)
        def _():
            for g_extra in range(nchunks, nchunks + NSLOT - 1):
                wait(lax.rem(b * nchunks + g_extra, NSLOT))

    qlf, qrf = ql.astype(F32), qr.astype(F32)
    cn, kn = cnew_ref[...], knew_ref[...]
    trow = lax.shift_right_logical(lax.broadcasted_iota(I32, (rows, 1), 0), int(math.log2(heads)))
    sn = []
    for t in range(tnew):
        st = (jnp.sum(qlf * cn[t:t + 1, :], axis=1, keepdims=True)
              + jnp.sum(qrf * kn[t:t + 1, :], axis=1, keepdims=True))
        sn.append(jnp.where(trow >= t, st, -jnp.inf))
    m2 = m1
    for t in range(tnew):
        m2 = jnp.maximum(m2, sn[t])
    a2 = jnp.exp2(m1 - m2)
    l2 = a2 * l1
    acc = a2 * acc1
    for t in range(tnew):
        pt = jnp.exp2(sn[t] - m2)
        l2 = l2 + pt
        acc = acc + pt * cn[t:t + 1, :]
    o_ref[...] = acc / l2


def _attn_sample(page_table, qlat, qatt, ckv_new, kr_new, cache_ckv, cache_krope, layer, *, heads, tnew, dn):
    nseq, npages = page_table.shape
    psize, kv = cache_ckv.shape[2], cache_ckv.shape[3]
    dr = cache_krope.shape[3]
    pages = math.gcd(npages, SAMPLE_PAGES)
    nchunks = npages // pages
    krope_t = jnp.swapaxes(cache_krope, 2, 3)
    rows = heads * tnew
    q3 = qlat.reshape(nseq, rows, kv)
    qa3 = qatt.reshape(nseq, rows, LANES)
    c3 = ckv_new.reshape(nseq, tnew, kv)
    k3 = kr_new.reshape(nseq, tnew, dr)
    in_specs = [pl.BlockSpec((None, rows, kv), lambda b, pt: (b, 0, 0)),
                pl.BlockSpec((None, rows, LANES), lambda b, pt: (b, 0, 0)),
                pl.BlockSpec((None, tnew, kv), lambda b, pt: (b, 0, 0)),
                pl.BlockSpec((None, tnew, dr), lambda b, pt: (b, 0, 0)),
                pl.BlockSpec(memory_space=pl.ANY), pl.BlockSpec(memory_space=pl.ANY)]
    out = pl.pallas_call(
        functools.partial(_attn_sample_kernel, layer=layer, pages=pages, psize=psize, nchunks=nchunks, nseq=nseq,
                          heads=heads, tnew=tnew, dn=dn, dr=dr),
        grid_spec=pltpu.PrefetchScalarGridSpec(
            num_scalar_prefetch=1, grid=(nseq,), in_specs=in_specs,
            out_specs=pl.BlockSpec((None, rows, kv), lambda b, pt: (b, 0, 0)),
            scratch_shapes=[pltpu.VMEM((NSLOT, pages * psize, kv), F32), pltpu.VMEM((NSLOT, dr, pages * psize), F32),
                            pltpu.SemaphoreType.DMA((2, NSLOT))]),
        out_shape=jax.ShapeDtypeStruct((nseq, rows, kv), F32),
        compiler_params=_cp("arbitrary"), name="attn_sample")(page_table, q3, qa3, c3, k3, cache_ckv, krope_t)
    return out.reshape(nseq * tnew, heads * kv)


def _head_blocks(w, dh, dtype):
    sh = int(math.log2(dh))
    r = lax.shift_right_logical(lax.broadcasted_iota(I32, (w, w), 0), sh)
    c = lax.shift_right_logical(lax.broadcasted_iota(I32, (w, w), 1), sh)
    return jnp.where(r == c, 1.0, 0.0).astype(dtype)


def _hgrn_intra_kernel(q_ref, lf_ref, k_ref, v_ref, oi_ref, qd_ref, kd_ref, ea_ref, a_sc, *, c, dh):
    tm, w = q_ref.shape
    sh = int(math.log2(c))
    r = lax.broadcasted_iota(I32, (tm, tm), 0)
    cc = lax.broadcasted_iota(I32, (tm, tm), 1)
    same = lax.shift_right_logical(r, sh) == lax.shift_right_logical(cc, sh)
    tril = jnp.where(same & (cc <= r), 1.0, 0.0).astype(BF16)
    blk = jnp.where(same, 1.0, 0.0).astype(BF16)
    lf = lf_ref[...]
    hi = lf.astype(BF16)
    r1 = lf - hi.astype(F32)
    mid = r1.astype(BF16)
    lo = (r1 - mid.astype(F32)).astype(BF16)
    a = _dot(tril, hi) + _dot(tril, mid) + _dot(tril, lo)
    atot = _dot(blk, hi) + _dot(blk, mid) + _dot(blk, lo)
    a_sc[...] = a
    qd_ref[...] = q_ref[...] * jnp.exp(a)
    kd_ref[...] = k_ref[...] * jnp.exp(atot - a)
    ea_ref[...] = jnp.exp(atot)
    bones_b = _head_blocks(w, dh, BF16)
    srow = lax.broadcasted_iota(I32, (c, 1), 0)

    def chunk(ci, carry):
        sl = pl.ds(pl.multiple_of(ci * c, c), c)
        ac, qc, kc, vc = a_sc[sl, :], q_ref[sl, :], k_ref[sl, :], v_ref[sl, :]
        ns = [8 if (c > 8 and t < 8) else c for t in range(c)]
        ws = []
        for t in range(c):
            n = ns[t]
            d = jnp.where(srow[0:n] <= t, ac[t:t + 1, :] - ac[0:n, :], -jnp.inf)
            ws.append(jnp.exp(d) * (qc[t:t + 1, :] * kc[0:n, :]))
        p = _dot(jnp.concatenate(ws, axis=0).astype(BF16), bones_b)
        outs, off = [], 0
        for t in range(c):
            outs.append(jnp.sum(p[off:off + ns[t], :] * vc[0:ns[t], :], axis=0, keepdims=True))
            off += ns[t]
        oi_ref[sl, :] = jnp.concatenate(outs, axis=0)
        return carry
    lax.fori_loop(0, tm // c, chunk, 0, unroll=2)


def _hgrn_seq_kernel(oi_ref, qd_ref, kd_ref, v_ref, g_ref, ea_ref, s0_ref, ng_ref, o_ref, sfin_ref, st_sc,
                     *, bt, nsteps, dh, unroll):
    j = pl.program_id(1)
    w = qd_ref.shape[-1]
    heads = w // dh
    bones = _head_blocks(w, dh, F32)
    bones_b = bones.astype(BF16)

    @pl.when(j == 0)
    def _():
        def init(b, carry):
            s0 = s0_ref[b]
            st_sc[b] = (jnp.concatenate([s0] * heads, axis=1) * bones).T
            return carry
        lax.fori_loop(0, bt, init, 0)

    ng = ng_ref[...]

    def step(b, carry):
        st = st_sc[b]
        g = g_ref[b]
        o = oi_ref[b] + lax.dot_general(qd_ref[b].astype(BF16), st.astype(BF16), _NT, preferred_element_type=F32)
        upd = lax.dot_general(v_ref[b].astype(BF16), kd_ref[b].astype(BF16), _TN, preferred_element_type=F32)
        st_sc[b] = st * ea_ref[b][0:1, :] + upd * bones
        ms = _dot((o * o).astype(BF16), bones_b) * (1.0 / dh)
        o_ref[b] = o * lax.rsqrt(ms + EPS) * ng * (g * jax.nn.sigmoid(g))
        return carry
    lax.fori_loop(0, bt, step, 0, unroll=unroll)

    @pl.when(j == nsteps - 1)
    def _():
        def fin(b, carry):
            sbd = st_sc[b].T
            acc = sbd[:, 0:dh]
            for h in range(1, heads):
                acc = acc + sbd[:, h * dh:(h + 1) * dh]
            sfin_ref[b] = acc
            return carry
        lax.fori_loop(0, bt, fin, 0)


def _hgrn(q, lf, k, v, g, s0, norm_g, *, c, bt, dh):
    batch, t, w = q.shape
    rows = batch * t
    tm = math.gcd(rows, TM)
    flat = lambda a: a.reshape(rows, w)
    tile = pl.BlockSpec((tm, w), lambda i: (i, 0))
    oi, qd, kd, ea = pl.pallas_call(
        functools.partial(_hgrn_intra_kernel, c=c, dh=dh),
        grid=(rows // tm,),
        in_specs=[tile] * 4, out_specs=[tile] * 4,
        out_shape=[jax.ShapeDtypeStruct((rows, w), F32)] * 4,
        scratch_shapes=[pltpu.VMEM((tm, w), F32)],
        compiler_params=_cp("arbitrary"), name="hgrn_intra")(flat(q), flat(lf), flat(k), flat(v))
    nsteps = t // c
    tok = pl.BlockSpec((bt, c, w), lambda bi, j: (bi, j, 0))
    st = pl.BlockSpec((bt, w, dh), lambda bi, j: (bi, 0, 0))
    b3 = lambda a: a.reshape(batch, t, w)
    return pl.pallas_call(
        functools.partial(_hgrn_seq_kernel, bt=bt, nsteps=nsteps, dh=dh, unroll=4),
        grid=(batch // bt, nsteps),
        in_specs=[tok] * 6 + [st, _full((1, w))],
        out_specs=[tok, st],
        out_shape=[jax.ShapeDtypeStruct((batch, t, w), F32), jax.ShapeDtypeStruct((batch, w, dh), F32)],
        scratch_shapes=[pltpu.VMEM((bt, w, w), F32)],
        compiler_params=_cp("arbitrary", "arbitrary"), name="hgrn_seq")(
            b3(oi), b3(qd), b3(kd), v, g, b3(ea), s0, norm_g)


def _conv_kernel(glu_ref, buf_ref, w_ref, cb_ref, lg_ref, lb_ref, o_ref, nbuf_ref, ext_sc, *, bt, tt, taps, nt):
    t = pl.program_id(1)
    hist = CONV_HIST
    base = hist - (taps - 1)
    for bi in range(bt):
        @pl.when(t == 0)
        def _():
            ext_sc[bi, 0:hist, :] = buf_ref[bi]

        ext_sc[bi, hist:hist + tt, :] = glu_ref[bi]
        acc = w_ref[0:1, :] * ext_sc[bi, base:base + tt, :]
        for j in range(1, taps):
            acc = acc + w_ref[j:j + 1, :] * ext_sc[bi, base + j:base + j + tt, :]
        y = _ln(acc + cb_ref[...], lg_ref[...], lb_ref[...])
        o_ref[bi] = y * jax.nn.sigmoid(y)

        @pl.when(t == nt - 1)
        def _():
            nbuf_ref[bi] = ext_sc[bi, tt + base:tt + hist, :]

        ext_sc[bi, 0:hist, :] = ext_sc[bi, tt:tt + hist, :]


def _conv(glu, buf_padded, w, cb, lg, lb, *, tt, bt):
    batch, t, ch = glu.shape
    taps = w.shape[0]
    nt = t // tt
    return pl.pallas_call(
        functools.partial(_conv_kernel, bt=bt, tt=tt, taps=taps, nt=nt),
        grid=(batch // bt, nt),
        in_specs=[pl.BlockSpec((bt, tt, ch), lambda b, i: (b, i, 0)),
                  pl.BlockSpec((bt, CONV_HIST, ch), lambda b, i: (b, 0, 0)),
                  _full(w.shape), _full((1, ch)), _full((1, ch)), _full((1, ch))],
        out_specs=[pl.BlockSpec((bt, tt, ch), lambda b, i: (b, i, 0)),
                   pl.BlockSpec((bt, taps - 1, ch), lambda b, i: (b, 0, 0))],
        out_shape=[jax.ShapeDtypeStruct((batch, t, ch), F32), jax.ShapeDtypeStruct((batch, taps - 1, ch), F32)],
        scratch_shapes=[pltpu.VMEM((bt, CONV_HIST + tt, ch), F32)],
        compiler_params=_cp("arbitrary", "arbitrary"), name="conv")(glu, buf_padded, w, cb, lg, lb)


def _outproj_kernel(hp_ref, hs_ref, oa_ref, ob_ref, oc_ref, olat_ref, obs_ref, ocs_ref, wuv_ref, wout_ref, g_ref, b_ref,
                    wr_ref, br_ref, h1_ref, route_ref, routet_ref, mix_sc, *, ntp, alpha, heads, kv, dv, ne, ng):
    i = pl.program_id(0)
    da = heads * dv
    wb = ob_ref.shape[1]

    @pl.when(i < ntp)
    def _():
        mix_sc[...] = (alpha * hp_ref[...] + _dot(oa_ref[...], wout_ref[0:da, :])
                       + _dot(ob_ref[...].astype(BF16), wout_ref[da:da + wb, :])
                       + _dot(oc_ref[...].astype(BF16), wout_ref[da + wb:, :]))

    @pl.when(i >= ntp)
    def _():
        mix = (alpha * hs_ref[...] + _dot(obs_ref[...].astype(BF16), wout_ref[da:da + wb, :])
               + _dot(ocs_ref[...].astype(BF16), wout_ref[da + wb:, :]))
        for h in range(heads):
            oh = _dot(olat_ref[:, h * kv:(h + 1) * kv].astype(BF16), wuv_ref[h])
            mix = mix + _dot(oh.astype(BF16), wout_ref[h * dv:(h + 1) * dv, :])
        mix_sc[...] = mix

    half = mix_sc.shape[0] // 2
    for rs in (slice(0, half), slice(half, 2 * half)):
        _norm_and_route(mix_sc, h1_ref, route_ref, routet_ref, g_ref, b_ref, wr_ref, br_ref, rs, ne=ne, ng=ng)


def _norm_and_route(mix_sc, h1_ref, route_ref, routet_ref, g_ref, b_ref, wr_ref, br_ref, rs, *, ne, ng):
    h1 = _ln(mix_sc[rs, :], g_ref[...], b_ref[...])
    h1_ref[rs, :] = h1
    h_hi = h1.astype(BF16)
    h_lo = (h1 - h_hi.astype(F32)).astype(BF16)
    part = _dot(h_hi, wr_ref[...])
    logits = (part[:, 0:LANES] + part[:, LANES:] + _dot(h_lo, wr_ref[:, 0:LANES])
              + br_ref[...])
    tm = logits.shape[0]
    lane = lax.broadcasted_iota(I32, (tm, LANES), 1).astype(F32)
    big = jnp.float32(1e9)
    epg = ne // ng
    is_g = (lane >= ne) & (lane < ne + ng)
    gl = jnp.where(is_g, logits, -jnp.inf)
    gmax = jnp.max(gl, axis=1, keepdims=True)
    gidx = jnp.min(jnp.where(gl == gmax, lane, big), axis=1, keepdims=True) - ne
    p_g = 1.0 / jnp.sum(jnp.where(is_g, jnp.exp(logits - gmax), 0.0), axis=1, keepdims=True)
    lo = gidx * epg
    el = jnp.where((lane >= lo) & (lane < lo + epg), logits, -jnp.inf)
    v1 = jnp.max(el, axis=1, keepdims=True)
    e1 = jnp.min(jnp.where(el == v1, lane, big), axis=1, keepdims=True)
    el2 = jnp.where(lane == e1, -jnp.inf, el)
    v2 = jnp.max(el2, axis=1, keepdims=True)
    e2 = jnp.min(jnp.where(el2 == v2, lane, big), axis=1, keepdims=True)
    tt = jnp.exp(v2 - v1)
    w1 = p_g / (1.0 + tt)
    w2 = w1 * tt
    route = jnp.where(lane == 0, e1, jnp.where(lane == 1, e2, jnp.where(lane == 2, w1, jnp.where(lane == 3, w2, 0.0))))
    route_ref[rs, :] = route
    routet_ref[:, rs] = route.T[0:8, :]


def _outproj(h_p, h_s, oa_p, ob_p, oc_p, olat_s, ob_s, oc_s, lw, *, ntp, nts, alpha):
    d = h_p.shape[1]
    heads, kv, dv = lw["heads"], lw["kv"], lw["dv"]
    nt = ntp + nts
    t_all = nt * TM

    def pspec(n):
        return pl.BlockSpec((TM, n), lambda i: (jnp.minimum(i, ntp - 1), 0))

    def sspec(n):
        return pl.BlockSpec((TM, n), lambda i: (jnp.maximum(i - ntp, 0), 0))

    tok = pl.BlockSpec((TM, d), lambda i: (i, 0))
    return pl.pallas_call(
        functools.partial(_outproj_kernel, ntp=ntp, alpha=alpha, heads=heads, kv=kv, dv=dv, ne=lw["ne"], ng=lw["ng"]),
        grid=(nt,),
        in_specs=[pspec(d), sspec(d), pspec(oa_p.shape[1]), pspec(ob_p.shape[1]), pspec(oc_p.shape[1]),
                  sspec(olat_s.shape[1]), sspec(ob_s.shape[1]), sspec(oc_s.shape[1]),
                  _full(lw["w_uv_h"].shape), _full(lw["w_out"].shape), _full((1, d)), _full((1, d)),
                  _full(lw["w_r"].shape), _full((1, LANES))],
        out_specs=[tok, pl.BlockSpec((TM, LANES), lambda i: (i, 0)), pl.BlockSpec((8, TM), lambda i: (0, i))],
        out_shape=[jax.ShapeDtypeStruct((t_all, d), F32), jax.ShapeDtypeStruct((t_all, LANES), F32),
                   jax.ShapeDtypeStruct((8, t_all), F32)],
        scratch_shapes=[pltpu.VMEM((TM, d), F32)],
        compiler_params=_cp("arbitrary"), name="outproj")(
            h_p, h_s, oa_p, ob_p, oc_p, olat_s, ob_s, oc_s, lw["w_uv_h"], lw["w_out"], lw["ln1_g"], lw["ln1_b"],
            lw["w_r"], lw["b_r"])


def _plan_kernel(rt_ref, dest_ref, be_ref, meta_ref, seg_ref, cnt_sc, seg_sc, carry_sc, *, bm, ne, nbp):
    p = pl.program_id(0)
    first = pl.program_id(1) == 0
    tm = rt_ref.shape[1]
    e1, e2 = rt_ref[0:1, :], rt_ref[1:2, :]
    sub = lax.broadcasted_iota(I32, (LANES, tm), 0).astype(F32)
    oh1 = sub == e1
    oh2 = sub == e2
    ohs = jnp.where(oh1, 1.0, 0.0) + jnp.where(oh2, 1.0, 0.0)
    tile_cnt = jnp.sum(ohs, axis=1, keepdims=True)

    @pl.when((p == 0) & first)
    def _():
        cnt_sc[...] = jnp.zeros_like(cnt_sc)

    @pl.when(p == 0)
    def _():
        cnt_sc[...] = cnt_sc[...] + tile_cnt

    @pl.when((p == 1) & first)
    def _():
        cnt = cnt_sc[...]
        padded = jnp.ceil(cnt * (1.0 / bm)) * bm
        r = lax.broadcasted_iota(I32, (LANES, LANES), 0)
        cidx = lax.broadcasted_iota(I32, (LANES, LANES), 1)
        lstrict = jnp.where(cidx < r, 1.0, 0.0).astype(F32)
        seg_start = _dot_hi(lstrict, jnp.broadcast_to(padded, (LANES, LANES)))[:, 0:1]
        seg_sc[...] = seg_start
        carry_sc[...] = jnp.zeros_like(carry_sc)
        seg_end = seg_start + padded
        jl = lax.broadcasted_iota(I32, (LANES, nbp), 1).astype(F32) * bm
        be = jnp.sum(jnp.where(seg_end <= jl, 1.0, 0.0), axis=0, keepdims=True)
        be_ref[...] = jnp.minimum(be, ne - 1).astype(I32)
        total = jnp.max(seg_end, axis=0, keepdims=True)
        meta_ref[...] = jnp.broadcast_to(total * (1.0 / bm), (1, LANES)).astype(I32)
        eye = r == cidx
        seg_ref[0:1, :] = jnp.sum(jnp.where(eye, seg_start, 0.0), axis=0, keepdims=True).astype(I32)
        seg_ref[1:2, :] = jnp.sum(jnp.where(eye, seg_end, 0.0), axis=0, keepdims=True).astype(I32)

    @pl.when(p == 1)
    def _():
        rr = lax.broadcasted_iota(I32, (tm, tm), 0)
        ccol = lax.broadcasted_iota(I32, (tm, tm), 1)
        ustrict = jnp.where(rr < ccol, 1.0, 0.0).astype(BF16)
        base = _dot(ohs.astype(BF16), ustrict) + carry_sc[...] + seg_sc[...]
        d1 = jnp.sum(jnp.where(oh1, base, 0.0), axis=0, keepdims=True)
        d2 = jnp.sum(jnp.where(oh2, base, 0.0), axis=0, keepdims=True)
        dest_ref[0:1, :] = d1.astype(I32)
        dest_ref[1:2, :] = d2.astype(I32)
        carry_sc[...] = carry_sc[...] + tile_cnt


def _plan(route_t, *, ne, nb_max):
    t_all = route_t.shape[1]
    tm = 512 if t_all % 512 == 0 else TM
    nt = t_all // tm
    nbp = pl.cdiv(nb_max, LANES) * LANES
    return pl.pallas_call(
        functools.partial(_plan_kernel, bm=MOE_BM, ne=ne, nbp=nbp),
        grid=(2, nt),
        in_specs=[pl.BlockSpec((8, tm), lambda p, i: (0, i))],
        out_specs=[pl.BlockSpec((2, tm), lambda p, i: (0, i * p)), _full((1, nbp)), _full((1, LANES)),
                   _full((2, LANES))],
        out_shape=[jax.ShapeDtypeStruct((2, t_all), I32), jax.ShapeDtypeStruct((1, nbp), I32),
                   jax.ShapeDtypeStruct((1, LANES), I32), jax.ShapeDtypeStruct((2, LANES), I32)],
        scratch_shapes=[pltpu.VMEM((LANES, 1), F32)] * 3,
        compiler_params=_cp("arbitrary", "arbitrary"), name="moe_plan")(route_t)


def _scatter_kernel(d1_ref, d2_ref, seg_ref, meta_ref, h_ref, rows_hbm, zbuf, sem, zsem, *, tm, ne, bm, nb):
    @pl.when(pl.program_id(0) == 0)
    def _():
        zbuf[...] = jnp.zeros_like(zbuf)
        for e in range(ne):
            @pl.when(seg_ref[1, e] > seg_ref[0, e])
            def _():
                last = pl.multiple_of(seg_ref[1, e] - bm, bm)
                pltpu.make_async_copy(zbuf, rows_hbm.at[pl.ds(last, bm)], zsem).start()

        def ztail(jb, carry):
            pltpu.make_async_copy(zbuf, rows_hbm.at[pl.ds(pl.multiple_of(jb * bm, bm), bm)], zsem).start()
            return carry
        lax.fori_loop(meta_ref[0, 0], nb, ztail, 0)
        for e in range(ne):
            @pl.when(seg_ref[1, e] > seg_ref[0, e])
            def _():
                pltpu.make_async_copy(zbuf, rows_hbm.at[pl.ds(0, bm)], zsem).wait()

        def zwait(jb, carry):
            pltpu.make_async_copy(zbuf, rows_hbm.at[pl.ds(0, bm)], zsem).wait()
            return carry
        lax.fori_loop(meta_ref[0, 0], nb, zwait, 0)

    for r in range(tm):
        src = h_ref.at[pl.ds(r, 1)]
        pltpu.make_async_copy(src, rows_hbm.at[pl.ds(d1_ref[0, 0, r], 1)], sem.at[0]).start(priority=0)
        pltpu.make_async_copy(src, rows_hbm.at[pl.ds(d2_ref[0, 0, r], 1)], sem.at[1]).start(priority=1)
    pltpu.make_async_copy(h_ref, rows_hbm.at[pl.ds(0, tm)], sem.at[0]).wait()
    pltpu.make_async_copy(h_ref, rows_hbm.at[pl.ds(0, tm)], sem.at[1]).wait()


def _scatter(h1, d1, d2, seg, meta, nrows, *, ne):
    t_all, d = h1.shape
    tm = d1.shape[2]
    dspec = pl.BlockSpec((1, 1, tm), lambda i: (i, 0, 0), memory_space=pltpu.SMEM)
    smem = pl.BlockSpec(memory_space=pltpu.SMEM)
    return pl.pallas_call(
        functools.partial(_scatter_kernel, tm=tm, ne=ne, bm=MOE_BM, nb=nrows // MOE_BM),
        grid=(t_all // tm,),
        in_specs=[dspec, dspec, smem, smem, pl.BlockSpec((tm, d), lambda i: (i, 0))],
        out_specs=pl.BlockSpec(memory_space=pl.ANY),
        out_shape=jax.ShapeDtypeStruct((nrows, d), F32),
        scratch_shapes=[pltpu.VMEM((MOE_BM, d), F32), pltpu.SemaphoreType.DMA((2,)), pltpu.SemaphoreType.DMA(())],
        compiler_params=_cp("arbitrary"), name="moe_scatter")(d1, d2, seg, meta, h1)


def _ffn_kernel(be_ref, na_ref, x_ref, wg_ref, wu_ref, wd_ref, y_ref, wgb, wub, wdb):
    i = pl.program_id(0)

    @pl.when(i < na_ref[0])
    def _():
        @pl.when((i == 0) | (be_ref[i] != be_ref[jnp.maximum(i - 1, 0)]))
        def _():
            wgb[...] = wg_ref[...].astype(BF16)
            wub[...] = wu_ref[...].astype(BF16)
            wdb[...] = wd_ref[...].astype(BF16)

        half = x_ref.shape[0] // 2
        for rs in (slice(0, half), slice(half, 2 * half)):
            x = x_ref[rs, :].astype(BF16)
            g = _dot(x, wgb[...])
            u = _dot(x, wub[...])
            hdn = (g * jax.nn.sigmoid(g) * u).astype(BF16)
            y_ref[rs, :] = _dot(hdn, wdb[...])

    @pl.when(i >= na_ref[0])
    def _():
        y_ref[...] = jnp.zeros_like(y_ref)


def _ffn(rows, be, nact, w_gate, w_up, w_down, layer):
    nrows, d = rows.shape
    ff = w_gate.shape[3]
    nb = nrows // MOE_BM

    def blk(i, be_r, na_r):
        return jnp.minimum(i, na_r[0] - 1)

    def wspec(a, b):
        return pl.BlockSpec((None, None, a, b), lambda i, be_r, na_r: (layer, be_r[blk(i, be_r, na_r)], 0, 0))

    row_spec = pl.BlockSpec((MOE_BM, d), lambda i, be_r, na_r: (blk(i, be_r, na_r), 0))
    return pl.pallas_call(
        _ffn_kernel,
        grid_spec=pltpu.PrefetchScalarGridSpec(
            num_scalar_prefetch=2, grid=(nb,),
            in_specs=[row_spec, wspec(d, ff), wspec(d, ff), wspec(ff, d)],
            out_specs=pl.BlockSpec((MOE_BM, d), lambda i, be_r, na_r: (i, 0)),
            scratch_shapes=[pltpu.VMEM((d, ff), BF16), pltpu.VMEM((d, ff), BF16), pltpu.VMEM((ff, d), BF16)]),
        out_shape=jax.ShapeDtypeStruct((nrows, d), F32),
        compiler_params=_cp("arbitrary"), name="moe_ffn")(be, nact, rows, w_gate, w_up, w_down)


def _combine_kernel(d1_ref, d2_ref, d1n_ref, d2n_ref, h1_ref, route_ref, yb_hbm, g_ref, b_ref, outp_ref, outs_ref,
                    buf1, buf2, sem, *, tm, nt, ntp, alpha):
    i = pl.program_id(0)
    slot = lax.rem(i, 2)

    def gather(da_ref, db_ref, s):
        for r in range(tm):
            pltpu.make_async_copy(yb_hbm.at[pl.ds(da_ref[0, 0, r], 1)], buf1.at[s, pl.ds(r, 1)],
                                  sem.at[0, s]).start(priority=0)
            pltpu.make_async_copy(yb_hbm.at[pl.ds(db_ref[0, 0, r], 1)], buf2.at[s, pl.ds(r, 1)],
                                  sem.at[1, s]).start(priority=1)

    @pl.when(i == 0)
    def _():
        gather(d1_ref, d2_ref, 0)

    @pl.when(i + 1 < nt)
    def _():
        gather(d1n_ref, d2n_ref, 1 - slot)

    pltpu.make_async_copy(yb_hbm.at[pl.ds(0, tm)], buf1.at[slot], sem.at[0, slot]).wait()
    pltpu.make_async_copy(yb_hbm.at[pl.ds(0, tm)], buf2.at[slot], sem.at[1, slot]).wait()
    moe = route_ref[:, 2:3] * buf1[slot] + route_ref[:, 3:4] * buf2[slot]
    out = _ln(alpha * h1_ref[...] + moe, g_ref[...], b_ref[...])

    @pl.when(i < ntp)
    def _():
        outp_ref[...] = out

    @pl.when(i >= ntp)
    def _():
        outs_ref[...] = out


def _combine(h1, route, yb, d1, d2, g, b, *, ntp, alpha):
    t_all, d = h1.shape
    tm = d1.shape[2]
    nt = t_all // tm
    dspec = pl.BlockSpec((1, 1, tm), lambda i: (i, 0, 0), memory_space=pltpu.SMEM)
    dnext = pl.BlockSpec((1, 1, tm), lambda i: (jnp.minimum(i + 1, nt - 1), 0, 0), memory_space=pltpu.SMEM)
    tok = pl.BlockSpec((tm, d), lambda i: (i, 0))
    return pl.pallas_call(
        functools.partial(_combine_kernel, tm=tm, nt=nt, ntp=ntp, alpha=alpha),
        grid=(nt,),
        in_specs=[dspec, dspec, dnext, dnext, tok, pl.BlockSpec((tm, LANES), lambda i: (i, 0)),
                  pl.BlockSpec(memory_space=pl.ANY), _full((1, d)), _full((1, d))],
        out_specs=[pl.BlockSpec((tm, d), lambda i: (jnp.minimum(i, ntp - 1), 0)),
                   pl.BlockSpec((tm, d), lambda i: (jnp.maximum(i - ntp, 0), 0))],
        out_shape=[jax.ShapeDtypeStruct((ntp * tm, d), F32), jax.ShapeDtypeStruct(((nt - ntp) * tm, d), F32)],
        scratch_shapes=[pltpu.VMEM((2, tm, d), F32), pltpu.VMEM((2, tm, d), F32), pltpu.SemaphoreType.DMA((2, 2))],
        compiler_params=_cp("arbitrary"), name="moe_combine")(d1, d2, d1, d2, h1, route, yb, g, b)


def _rope_tables(pos, dr, scale):
    half = dr // 2
    inv = ROPE_BASE ** (-jnp.arange(half, dtype=F32) / half)
    ang = pos.astype(F32)[:, None] * inv[None, :]
    cos, sin = jnp.cos(ang), jnp.sin(ang)
    n = pos.shape[0]
    z64, z32, o64 = jnp.zeros((n, 64), F32), jnp.zeros((n, 32), F32), jnp.ones((n, 64), F32)
    cq = jnp.concatenate([o64, cos, cos, z32], axis=1) * scale
    sq = jnp.concatenate([z64, -sin, sin, z32], axis=1) * scale
    ck = jnp.concatenate([z64, cos, cos, z32], axis=1)
    sk = jnp.concatenate([z64, -sin, sin, z32], axis=1)
    return cq, sq, ck, sk


def _prep_layer(l, p):
    w_in = p["w_in"][l]
    d = w_in.shape[0]
    ql, kv = p["mla_q_norm"].shape[1], p["mla_kv_norm"].shape[1]
    heads, dn = p["mla_w_uk"].shape[2], p["mla_w_uk"].shape[3]
    dv = p["mla_w_uv"].shape[3]
    dr = p["mla_w_uq"].shape[2] // heads - dn
    assert dn == 64 and dr == 32 and dv == 64 and ql % LANES == 0 and kv % LANES == 0
    hw = p["hgrn_lb_logits"].shape[1]
    ch = p["conv_b"].shape[1]
    assert hw == 256 and ch == 256
    o = ql + kv
    kr = w_in[:, o:o + dr]
    x1, x2 = kr[:, :dr // 2], kr[:, dr // 2:]
    seg = jnp.concatenate([jnp.zeros((d, 64), F32), x1, x2, x2, x1], axis=1)
    w_in_p = jnp.concatenate([w_in[:, :o], seg, w_in[:, o + dr:]], axis=1).astype(BF16)
    uq = p["mla_w_uq"][l].reshape(ql, heads, dn + dr)
    u1, u2 = uq[:, :, dn:dn + dr // 2], uq[:, :, dn + dr // 2:]
    w_uq = jnp.concatenate([uq[:, :, :dn], u1, u2, u2, u1], axis=2).reshape(ql, heads * LANES).astype(BF16)
    uk = p["mla_w_uk"][l]
    w_uk = jnp.concatenate([uk, jnp.zeros_like(uk)], axis=2).reshape(kv, heads * LANES).astype(BF16)
    w_ukt = jnp.transpose(uk, (1, 2, 0)).astype(BF16)
    uv = p["mla_w_uv"][l]
    lb = p["lb_all"][l]
    lbc = jnp.zeros((8, hw), F32).at[0].set(jnp.log(lb)).at[1].set(jnp.log1p(-lb)).at[2].set(1.0 - lb)
    ne, ng = p["router_e_w"].shape[2], p["router_g_w"].shape[2]
    w_r = jnp.concatenate([p["router_e_w"][l], p["router_g_w"][l], jnp.zeros((d, LANES - ne - ng), F32)], axis=1)
    b_r = jnp.concatenate([p["router_e_b"][l], p["router_g_b"][l], jnp.zeros((LANES - ne - ng,), F32)])[None, :]
    w_r_hi = w_r.astype(BF16)
    w_r = jnp.concatenate([w_r_hi, (w_r - w_r_hi.astype(F32)).astype(BF16)], axis=1)
    return dict(
        ql=ql, kv=kv, heads=heads, dn=dn, dv=dv, dr=dr, ne=ne, ng=ng,
        w_in=w_in_p, q_norm=p["mla_q_norm"][l][None, :], w_uq=w_uq, kv_norm=p["mla_kv_norm"][l][None, :],
        w_uk=w_uk, w_ukt=w_ukt, w_uv=uv.reshape(kv, heads * dv).astype(BF16),
        w_uv_h=jnp.transpose(uv, (1, 0, 2)).astype(BF16), lbc=lbc,
        hgrn_norm=jnp.tile(p["hgrn_norm"][l], hw // p["hgrn_norm"].shape[1])[None, :],
        conv_w=p["conv_w"][l], conv_b=p["conv_b"][l][None, :],
        conv_ln_g=p["conv_ln_g"][l][None, :], conv_ln_b=p["conv_ln_b"][l][None, :],
        w_out=p["w_out"][l].astype(BF16), ln1_g=p["ln1_g"][l][None, :], ln1_b=p["ln1_b"][l][None, :],
        ln2_g=p["ln2_g"][l][None, :], ln2_b=p["ln2_b"][l][None, :], w_r=w_r, b_r=b_r)


def kernel(x_prompt, x_sample, cache_ckv, cache_krope, state_hgrn, state_conv, page_table, w_in, mla_q_norm, mla_w_uq, mla_kv_norm, mla_w_uk, mla_w_uv, hgrn_lb_logits, hgrn_norm, conv_w, conv_b, conv_ln_g, conv_ln_b, w_out, ln1_g, ln1_b, ln2_g, ln2_b, router_g_w, router_g_b, router_e_w, router_e_b, exp_w_gate, exp_w_up, exp_w_down):
    batch, seq, d = x_prompt.shape
    nseq, tnew, _ = x_sample.shape
    depth = w_in.shape[0]
    t_p, t_s = batch * seq, nseq * tnew
    t_all = t_p + t_s
    assert t_p % TM == 0 and t_s % TM == 0 and seq % TM == 0 and seq % HGRN_C == 0
    ntp, nts = t_p // TM, t_s // TM
    past_len = page_table.shape[1] * cache_ckv.shape[2]
    alpha = (2 * depth) ** 0.25
    hb, dkb = state_hgrn.shape[2], state_hgrn.shape[3]
    taps = conv_w.shape[1]
    assert taps - 1 <= CONV_HIST

    lbp = jax.nn.softmax(hgrn_lb_logits.astype(F32), axis=0)
    lbcs = jnp.cumsum(lbp, axis=0)
    params = dict(w_in=w_in, mla_q_norm=mla_q_norm, mla_w_uq=mla_w_uq, mla_kv_norm=mla_kv_norm, mla_w_uk=mla_w_uk,
                  mla_w_uv=mla_w_uv, hgrn_lb_logits=hgrn_lb_logits, lb_all=lbcs - lbcs[0:1], hgrn_norm=hgrn_norm,
                  conv_w=conv_w, conv_b=conv_b, conv_ln_g=conv_ln_g, conv_ln_b=conv_ln_b, w_out=w_out,
                  ln1_g=ln1_g, ln1_b=ln1_b, ln2_g=ln2_g, ln2_b=ln2_b, router_g_w=router_g_w, router_g_b=router_g_b,
                  router_e_w=router_e_w, router_e_b=router_e_b)
    dn = mla_w_uk.shape[3]
    dr = cache_krope.shape[3]
    scale = (dn + dr) ** -0.5 * math.log2(math.e)
    tabs_p = _rope_tables(jnp.arange(seq), dr, scale)
    tabs_s = _rope_tables(jnp.tile(past_len + jnp.arange(tnew), nseq), dr, scale)

    ne = router_e_w.shape[2]
    n_assign = 2 * t_all
    nb_max = (n_assign + ne * (MOE_BM - 1) + MOE_BM - 1) // MOE_BM
    nrows = nb_max * MOE_BM

    h_p, h_s = x_prompt.reshape(t_p, d), x_sample.reshape(t_s, d)
    outs = {k: [] for k in ("ckv_p", "kr_p", "sh_p", "sc_p", "ckv_s", "kr_s", "sh_s", "sc_s")}
    hw = hb * dkb
    tpad = 8
    bt_s = math.gcd(nseq, 16)
    for l in range(depth):
        lw = _prep_layer(l, params)
        heads, kv, dv = lw["heads"], lw["kv"], lw["dv"]
        (qatt, katt, vatt, ckv_p, kr_p, hq, hlf, hk, hv, hg, glu) = _inproj(
            h_p, lw, tabs_p, tab_period=seq // TM, sample=False)
        oa_p = _attn_prompt(qatt, katt, vatt, batch=batch, seq=seq, heads=heads, dv=dv)
        r3 = lambda a: a.reshape(batch, seq, a.shape[1])
        ob_p, s_p = _hgrn(r3(hq), r3(hlf), r3(hk), r3(hv), r3(hg), jnp.zeros((batch, hw, dkb), F32), lw["hgrn_norm"],
                          c=HGRN_C, bt=batch, dh=dkb)
        oc_p, buf_p = _conv(r3(glu), jnp.zeros((batch, CONV_HIST, glu.shape[1]), F32), lw["conv_w"], lw["conv_b"],
                            lw["conv_ln_g"], lw["conv_ln_b"], tt=TM, bt=1)
        (qatt_s, qlat_s, ckv_s, kr_s, hq, hlf, hk, hv, hg, glu_s) = _inproj(
            h_s, lw, tabs_s, tab_period=nts, sample=True)
        olat_s = _attn_sample(page_table, qlat_s, qatt_s, ckv_s, kr_s, cache_ckv, cache_krope, l,
                              heads=heads, tnew=tnew, dn=lw["dn"])
        r3s = lambda a: jnp.pad(a.reshape(nseq, tnew, a.shape[1]), ((0, 0), (0, tpad - tnew), (0, 0)))
        ob_s, s_s = _hgrn(r3s(hq), r3s(hlf), r3s(hk), r3s(hv), r3s(hg), state_hgrn[l].reshape(nseq, hw, dkb),
                          lw["hgrn_norm"], c=tpad, bt=bt_s, dh=dkb)
        ob_s = ob_s[:, :tnew].reshape(t_s, hw)
        bufpad = jnp.pad(state_conv[l], ((0, 0), (CONV_HIST - (taps - 1), 0), (0, 0)))
        oc_s, buf_s = _conv(glu_s.reshape(nseq, tnew, -1), bufpad, lw["conv_w"], lw["conv_b"],
                            lw["conv_ln_g"], lw["conv_ln_b"], tt=tnew, bt=math.gcd(nseq, 8))
        h1, route, route_t = _outproj(h_p, h_s, oa_p, ob_p.reshape(t_p, hw), oc_p.reshape(t_p, -1), olat_s, ob_s,
                                      oc_s.reshape(t_s, -1), lw, ntp=ntp, nts=nts, alpha=alpha)
        dest, be, meta, seg = _plan(route_t, ne=ne, nb_max=nb_max)
        d1 = dest[0].reshape(t_all // TM, 1, TM)
        d2 = dest[1].reshape(t_all // TM, 1, TM)
        rows = _scatter(h1, d1, d2, seg, meta, nrows, ne=ne)
        yb = _ffn(rows, be.reshape(-1), meta[0, 0:1], exp_w_gate, exp_w_up, exp_w_down, l)
        h_p, h_s = _combine(h1, route, yb, d1, d2, lw["ln2_g"], lw["ln2_b"], ntp=ntp, alpha=alpha)

        outs["ckv_p"].append(ckv_p.reshape(batch, seq, kv))
        outs["kr_p"].append(kr_p.reshape(batch, seq, dr))
        outs["sh_p"].append(s_p.reshape(batch, hb, dkb, dkb))
        outs["sc_p"].append(buf_p)
        outs["ckv_s"].append(ckv_s.reshape(nseq, tnew, kv))
        outs["kr_s"].append(kr_s.reshape(nseq, tnew, dr))
        outs["sh_s"].append(s_s.reshape(nseq, hb, dkb, dkb))
        outs["sc_s"].append(buf_s)
    st = lambda k: jnp.stack(outs[k])
    return (h_p.reshape(batch, seq, d), h_s.reshape(nseq, tnew, d),
            st("ckv_p"), st("kr_p"), st("sh_p"), st("sc_p"), st("ckv_s"), st("kr_s"), st("sh_s"), st("sc_s"))
```

```python
import functools
import math

import jax
import jax.numpy as jnp
from jax import lax
from jax.experimental import pallas as pl
from jax.experimental.pallas import tpu as pltpu

F32 = jnp.float32
BF16 = jnp.bfloat16
I32 = jnp.int32

LANES = 128
ROPE_BASE = 10000.0
EPS = 1e-6
TM = 256
MOE_BM = 512
HGRN_C = 16
CONV_HIST = 32
SAMPLE_PAGES = 32
NSLOT = 4
VMEM_LIMIT = 48 * 1024 * 1024
_HI = lax.Precision.HIGHEST
_NT = (((1,), (1,)), ((), ()))
_TN = (((0,), (0,)), ((), ()))


def _cp(*sem):
    return pltpu.CompilerParams(dimension_semantics=sem, vmem_limit_bytes=VMEM_LIMIT)


def _dot(a, b):
    return jnp.dot(a, b, preferred_element_type=F32)


def _dot_hi(a, b):
    return jnp.dot(a, b, preferred_element_type=F32, precision=_HI)


def _rms(x, g):
    return x * lax.rsqrt(jnp.mean(x * x, axis=-1, keepdims=True) + EPS) * g


def _ln(x, g, b):
    mu = jnp.mean(x, axis=-1, keepdims=True)
    xc = x - mu
    var = jnp.mean(xc * xc, axis=-1, keepdims=True)
    return xc * lax.rsqrt(var + EPS) * g + b


def _full(shape):
    n = len(shape)
    return pl.BlockSpec(shape, lambda *_: (0,) * n)


def _proj_common(x, win_ref, qn_ref, wuq_ref, kvn_ref, cq_ref, sq_ref, ck_ref, sk_ref, lb_ref, *, ql, kv, heads):
    o_kr = ql + kv
    o_h = o_kr + LANES
    o_c = o_h + 4 * 256
    u = _dot(x, win_ref[...])
    y = _rms(u[:, 0:ql], qn_ref[...]).astype(BF16)
    q = _dot(y, wuq_ref[...])
    cqt, sqt = cq_ref[...], sq_ref[...]
    qg = []
    for g in range(heads):
        t = q[:, g * LANES:(g + 1) * LANES]
        qg.append(t * cqt + pltpu.roll(t, 96, axis=1) * sqt)
    ckv = _rms(u[:, ql:o_kr], kvn_ref[...])
    kr = u[:, o_kr:o_h]
    kr_rot = kr * ck_ref[...] + pltpu.roll(kr, 96, axis=1) * sk_ref[...]
    uh = u[:, o_h:o_c]
    hq, z, hv, hg = uh[:, 0:256], uh[:, 256:512], uh[:, 512:768], uh[:, 768:1024]
    log_lb, log1m_lb, one_m_lb = lb_ref[0:1, :], lb_ref[1:2, :], lb_ref[2:3, :]
    log_sig = jnp.minimum(z, 0.0) - jnp.log1p(jnp.exp(-jnp.abs(z)))
    b = log1m_lb + log_sig
    hlf = jnp.maximum(log_lb, b) + jnp.log1p(jnp.exp(-jnp.abs(log_lb - b)))
    hk = one_m_lb * jax.nn.sigmoid(-z)
    uc = u[:, o_c:o_c + 512]
    glu = uc[:, 0:256] * jax.nn.sigmoid(uc[:, 256:512])
    return qg, ckv, kr_rot, (hq, hlf, hk, hv, hg), glu


def _inproj_prompt_kernel(h_ref, win_ref, qn_ref, wuq_ref, kvn_ref, cq_ref, sq_ref, ck_ref, sk_ref, lb_ref,
                          wuk_ref, wuv_ref,
                          qatt_ref, katt_ref, vatt_ref, ckv_ref, kr_ref, hq_ref, hlf_ref, hk_ref, hv_ref, hg_ref,
                          glu_ref, *, ql, kv, heads):
    x = h_ref[...].astype(BF16)
    qg, ckv, kr_rot, hg5, glu = _proj_common(x, win_ref, qn_ref, wuq_ref, kvn_ref, cq_ref, sq_ref, ck_ref, sk_ref,
                                             lb_ref, ql=ql, kv=kv, heads=heads)
    cb = ckv.astype(BF16)
    kn = _dot(cb, wuk_ref[...])
    for g in range(heads):
        sl = slice(g * LANES, (g + 1) * LANES)
        qatt_ref[:, sl] = qg[g].astype(BF16)
        katt_ref[:, sl] = (kn[:, sl] + kr_rot).astype(BF16)
    vatt_ref[...] = _dot(cb, wuv_ref[...]).astype(BF16)
    ckv_ref[...] = ckv
    kr_ref[...] = pltpu.roll(kr_rot, 64, axis=1)[:, 0:32]
    for r, v in zip((hq_ref, hlf_ref, hk_ref, hv_ref, hg_ref), hg5):
        r[...] = v
    glu_ref[...] = glu


def _inproj_sample_kernel(h_ref, win_ref, qn_ref, wuq_ref, kvn_ref, cq_ref, sq_ref, ck_ref, sk_ref, lb_ref,
                          wukt_ref,
                          qatt_ref, qlat_ref, ckv_ref, kr_ref, hq_ref, hlf_ref, hk_ref, hv_ref, hg_ref,
                          glu_ref, *, ql, kv, heads, dn):
    x = h_ref[...].astype(BF16)
    qg, ckv, kr_rot, hg5, glu = _proj_common(x, win_ref, qn_ref, wuq_ref, kvn_ref, cq_ref, sq_ref, ck_ref, sk_ref,
                                             lb_ref, ql=ql, kv=kv, heads=heads)
    for g in range(heads):
        qatt_ref[:, g * LANES:(g + 1) * LANES] = qg[g].astype(BF16)
        qn = qg[g][:, 0:dn].astype(BF16)
        qlat_ref[:, g * kv:(g + 1) * kv] = _dot(qn, wukt_ref[g]).astype(BF16)
    ckv_ref[...] = ckv
    kr_ref[...] = pltpu.roll(kr_rot, 64, axis=1)[:, 0:32]
    for r, v in zip((hq_ref, hlf_ref, hk_ref, hv_ref, hg_ref), hg5):
        r[...] = v
    glu_ref[...] = glu


def _inproj(h, lw, tabs, *, tab_period, sample):
    t, d = h.shape
    ntiles = t // TM
    ql, kv, heads = lw["ql"], lw["kv"], lw["heads"]
    tab_spec = pl.BlockSpec((TM, LANES), lambda i: (i % tab_period, 0))
    in_specs = [pl.BlockSpec((TM, d), lambda i: (i, 0)),
                _full(lw["w_in"].shape), _full((1, ql)), _full(lw["w_uq"].shape), _full((1, kv)),
                tab_spec, tab_spec, tab_spec, tab_spec, _full((8, 256))]
    args = [h, lw["w_in"], lw["q_norm"], lw["w_uq"], lw["kv_norm"], *tabs, lw["lbc"]]

    def tok(n, dt):
        return jax.ShapeDtypeStruct((t, n), dt), pl.BlockSpec((TM, n), lambda i: (i, 0))

    tail = [tok(kv, F32), tok(32, F32)] + [tok(256, F32)] * 6
    if sample:
        in_specs += [_full(lw["w_ukt"].shape)]
        args += [lw["w_ukt"]]
        outs = [tok(heads * LANES, BF16), tok(heads * kv, BF16)] + tail
        body = functools.partial(_inproj_sample_kernel, ql=ql, kv=kv, heads=heads, dn=lw["dn"])
        name = "inproj_sample"
    else:
        in_specs += [_full(lw["w_uk"].shape), _full(lw["w_uv"].shape)]
        args += [lw["w_uk"], lw["w_uv"]]
        outs = [tok(heads * LANES, BF16), tok(heads * LANES, BF16), tok(lw["w_uv"].shape[1], BF16)] + tail
        body = functools.partial(_inproj_prompt_kernel, ql=ql, kv=kv, heads=heads)
        name = "inproj_prompt"
    return pl.pallas_call(
        body, grid=(ntiles,), in_specs=in_specs,
        out_specs=[o[1] for o in outs], out_shape=[o[0] for o in outs],
        compiler_params=_cp("arbitrary"), name=name)(*args)


def _attn_prompt_kernel(q_ref, k_ref, v_ref, o_ref, *, seq, tq, dv):
    key = lax.broadcasted_iota(I32, (tq, tq), 0)
    qry = lax.broadcasted_iota(I32, (tq, tq), 1)
    causal = key <= qry
    vall = v_ref[...]
    vh = [vall[:, hh * dv:(hh + 1) * dv] for hh in range(2)]
    for qi in range(seq // tq):
        q0, q1 = qi * tq, (qi + 1) * tq
        outs = []
        for hh in range(2):
            hs = slice(hh * LANES, (hh + 1) * LANES)
            q = q_ref[q0:q1, hs]
            sd = lax.dot_general(k_ref[q0:q1, hs], q, _NT, preferred_element_type=F32)
            sd = jnp.where(causal, sd, -jnp.inf)
            m = jnp.max(sd, axis=0, keepdims=True)
            if qi > 0:
                so = lax.dot_general(k_ref[0:q0, hs], q, _NT, preferred_element_type=F32)
                m = jnp.maximum(m, jnp.max(so, axis=0, keepdims=True))
                po = jnp.exp2(so - m)
            pd = jnp.exp2(sd - m)
            l = jnp.sum(pd, axis=0, keepdims=True)
            o = lax.dot_general(vh[hh][q0:q1, :], pd.astype(BF16), _TN, preferred_element_type=F32)
            if qi > 0:
                l = l + jnp.sum(po, axis=0, keepdims=True)
                o = o + lax.dot_general(vh[hh][0:q0, :], po.astype(BF16), _TN, preferred_element_type=F32)
            outs.append(o / l)
        o_ref[q0:q1, :] = jnp.concatenate(outs, axis=0).T.astype(o_ref.dtype)


def _attn_prompt(qatt, katt, vatt, *, batch, seq, heads, dv):
    assert heads % 2 == 0 and 2 * dv == LANES
    tq = min(512, seq)
    q3 = qatt.reshape(batch, seq, heads * LANES)
    k3 = katt.reshape(batch, seq, heads * LANES)
    v3 = vatt.reshape(batch, seq, heads * dv)
    out = pl.pallas_call(
        functools.partial(_attn_prompt_kernel, seq=seq, tq=tq, dv=dv),
        grid=(batch, heads // 2),
        in_specs=[pl.BlockSpec((None, seq, 2 * LANES), lambda b, h: (b, 0, h)),
                  pl.BlockSpec((None, seq, 2 * LANES), lambda b, h: (b, 0, h)),
                  pl.BlockSpec((None, seq, LANES), lambda b, h: (b, 0, h))],
        out_specs=pl.BlockSpec((None, seq, LANES), lambda b, h: (b, 0, h)),
        out_shape=jax.ShapeDtypeStruct((batch, seq, heads * dv), BF16),
        compiler_params=_cp("arbitrary", "arbitrary"), name="attn_prompt")(q3, k3, v3)
    return out.reshape(batch * seq, heads * dv)


def _attn_sample_kernel(pt_ref, qlat_ref, qatt_ref, cnew_ref, knew_ref, ckv_hbm, krt_hbm, o_ref, kbuf, rbuf, sem,
                        *, layer, pages, psize, nchunks, nseq, heads, tnew, dn, dr):
    b = pl.program_id(0)

    def fetch(seq, chunk, slot):
        for i in range(pages):
            page = pt_ref[seq, chunk * pages + i]
            keys = pl.ds(i * psize, psize)
            pltpu.make_async_copy(ckv_hbm.at[layer, page], kbuf.at[slot, keys], sem.at[0, slot]).start()
            pltpu.make_async_copy(krt_hbm.at[layer, page], rbuf.at[slot, :, keys], sem.at[1, slot]).start()

    unguarded = nchunks >= NSLOT - 1

    def fetch_ahead(g_next):
        step_ahead, chunk = divmod(g_next, nchunks)
        slot = lax.rem(b * nchunks + g_next, NSLOT)
        if step_ahead == 0:
            fetch(b, chunk, slot)
        elif unguarded:
            fetch(jnp.minimum(b + step_ahead, nseq - 1), chunk, slot)
        else:
            pl.when(b + step_ahead < nseq)(lambda: fetch(b + step_ahead, chunk, slot))

    @pl.when(b == 0)
    def _():
        for g0 in range(NSLOT - 1):
            fetch_ahead(g0)

    ql = qlat_ref[...]
    qr = qatt_ref[:, dn:dn + dr]
    rows, kv = ql.shape

    def wait(slot):
        pltpu.make_async_copy(kbuf.at[slot], kbuf.at[slot], sem.at[0, slot]).wait()
        pltpu.make_async_copy(rbuf.at[slot], rbuf.at[slot], sem.at[1, slot]).wait()

    def scores(slot):
        kc = kbuf[slot].astype(BF16)
        krt = rbuf[slot].astype(BF16)
        return lax.dot_general(ql, kc, _NT, preferred_element_type=F32) + _dot(qr, krt)

    slot0 = lax.rem(b * nchunks, NSLOT)
    wait(slot0)
    s_cur = scores(slot0)
    m1 = jnp.full((rows, 1), -jnp.inf, F32)
    l1 = jnp.zeros((rows, 1), F32)
    acc1 = jnp.zeros((rows, kv), F32)
    for c in range(nchunks):
        slot = lax.rem(b * nchunks + c, NSLOT)
        fetch_ahead(c + NSLOT - 1)
        if c + 1 < nchunks:
            slot_n = lax.rem(b * nchunks + c + 1, NSLOT)
            wait(slot_n)
            s_next = scores(slot_n)
        m_new = jnp.maximum(m1, jnp.max(s_cur, axis=1, keepdims=True))
        alpha = jnp.exp2(m1 - m_new)
        p = jnp.exp2(s_cur - m_new)
        l1 = alpha * l1 + jnp.sum(p, axis=1, keepdims=True)
        acc1 = alpha * acc1 + _dot(p.astype(BF16), kbuf[slot].astype(BF16))
        m1 = m_new
        if c + 1 < nchunks:
            s_cur = s_next

    if unguarded:
        @pl.when(b == nseq - 1)
        def _():
            for g_extra in range(nchunks, nchunks + NSLOT - 1):
                wait(lax.rem(b * nchunks + g_extra, NSLOT))

    qlf, qrf = ql.astype(F32), qr.astype(F32)
    cn, kn = cnew_ref[...], knew_ref[...]
    trow = lax.shift_right_logical(lax.broadcasted_iota(I32, (rows, 1), 0), int(math.log2(heads)))
    sn = []
    for t in range(tnew):
        st = (jnp.sum(qlf * cn[t:t + 1, :], axis=1, keepdims=True)
              + jnp.sum(qrf * kn[t:t + 1, :], axis=1, keepdims=True))
        sn.append(jnp.where(trow >= t, st, -jnp.inf))
    m2 = m1
    for t in range(tnew):
        m2 = jnp.maximum(m2, sn[t])
    a2 = jnp.exp2(m1 - m2)
    l2 = a2 * l1
    acc = a2 * acc1
    for t in range(tnew):
        pt = jnp.exp2(sn[t] - m2)
        l2 = l2 + pt
        acc = acc + pt * cn[t:t + 1, :]
    o_ref[...] = acc / l2


def _attn_sample(page_table, qlat, qatt, ckv_new, kr_new, cache_ckv, cache_krope, layer, *, heads, tnew, dn):
    nseq, npages = page_table.shape
    psize, kv = cache_ckv.shape[2], cache_ckv.shape[3]
    dr = cache_krope.shape[3]
    pages = math.gcd(npages, SAMPLE_PAGES)
    nchunks = npages // pages
    krope_t = jnp.swapaxes(cache_krope, 2, 3)
    rows = heads * tnew
    q3 = qlat.reshape(nseq, rows, kv)
    qa3 = qatt.reshape(nseq, rows, LANES)
    c3 = ckv_new.reshape(nseq, tnew, kv)
    k3 = kr_new.reshape(nseq, tnew, dr)
    in_specs = [pl.BlockSpec((None, rows, kv), lambda b, pt: (b, 0, 0)),
                pl.BlockSpec((None, rows, LANES), lambda b, pt: (b, 0, 0)),
                pl.BlockSpec((None, tnew, kv), lambda b, pt: (b, 0, 0)),
                pl.BlockSpec((None, tnew, dr), lambda b, pt: (b, 0, 0)),
                pl.BlockSpec(memory_space=pl.ANY), pl.BlockSpec(memory_space=pl.ANY)]
    out = pl.pallas_call(
        functools.partial(_attn_sample_kernel, layer=layer, pages=pages, psize=psize, nchunks=nchunks, nseq=nseq,
                          heads=heads, tnew=tnew, dn=dn, dr=dr),
        grid_spec=pltpu.PrefetchScalarGridSpec(
            num_scalar_prefetch=1, grid=(nseq,), in_specs=in_specs,
            out_specs=pl.BlockSpec((None, rows, kv), lambda b, pt: (b, 0, 0)),
            scratch_shapes=[pltpu.VMEM((NSLOT, pages * psize, kv), F32), pltpu.VMEM((NSLOT, dr, pages * psize), F32),
                            pltpu.SemaphoreType.DMA((2, NSLOT))]),
        out_shape=jax.ShapeDtypeStruct((nseq, rows, kv), F32),
        compiler_params=_cp("arbitrary"), name="attn_sample")(page_table, q3, qa3, c3, k3, cache_ckv, krope_t)
    return out.reshape(nseq * tnew, heads * kv)


def _head_blocks(w, dh, dtype):
    sh = int(math.log2(dh))
    r = lax.shift_right_logical(lax.broadcasted_iota(I32, (w, w), 0), sh)
    c = lax.shift_right_logical(lax.broadcasted_iota(I32, (w, w), 1), sh)
    return jnp.where(r == c, 1.0, 0.0).astype(dtype)


def _hgrn_intra_kernel(q_ref, lf_ref, k_ref, v_ref, oi_ref, qd_ref, kd_ref, ea_ref, a_sc, *, c, dh):
    tm, w = q_ref.shape
    sh = int(math.log2(c))
    r = lax.broadcasted_iota(I32, (tm, tm), 0)
    cc = lax.broadcasted_iota(I32, (tm, tm), 1)
    same = lax.shift_right_logical(r, sh) == lax.shift_right_logical(cc, sh)
    tril = jnp.where(same & (cc <= r), 1.0, 0.0).astype(BF16)
    blk = jnp.where(same, 1.0, 0.0).astype(BF16)
    lf = lf_ref[...]
    hi = lf.astype(BF16)
    r1 = lf - hi.astype(F32)
    mid = r1.astype(BF16)
    lo = (r1 - mid.astype(F32)).astype(BF16)
    log2e = math.log2(math.e)
    a = (_dot(tril, hi) + _dot(tril, mid) + _dot(tril, lo)) * log2e
    atot = (_dot(blk, hi) + _dot(blk, mid) + _dot(blk, lo)) * log2e
    a_sc[...] = a
    qd_ref[...] = q_ref[...] * jnp.exp2(a)
    kd_ref[...] = k_ref[...] * jnp.exp2(atot - a)
    ea_ref[...] = jnp.exp2(atot)
    bones_b = _head_blocks(w, dh, BF16)
    srow = lax.broadcasted_iota(I32, (c, 1), 0)

    def chunk(ci, carry):
        sl = pl.ds(pl.multiple_of(ci * c, c), c)
        ac, qc, kc, vc = a_sc[sl, :], q_ref[sl, :], k_ref[sl, :], v_ref[sl, :]
        ns = [8 if (c > 8 and t < 8) else c for t in range(c)]
        ws = []
        for t in range(c):
            n = ns[t]
            d = jnp.where(srow[0:n] <= t, ac[t:t + 1, :] - ac[0:n, :], -jnp.inf)
            ws.append(jnp.exp2(d) * (qc[t:t + 1, :] * kc[0:n, :]))
        p = _dot(jnp.concatenate(ws, axis=0).astype(BF16), bones_b)
        outs, off = [], 0
        for t in range(c):
            outs.append(jnp.sum(p[off:off + ns[t], :] * vc[0:ns[t], :], axis=0, keepdims=True))
            off += ns[t]
        oi_ref[sl, :] = jnp.concatenate(outs, axis=0)
        return carry
    lax.fori_loop(0, tm // c, chunk, 0, unroll=2)


def _hgrn_seq_kernel(oi_ref, qd_ref, kd_ref, v_ref, g_ref, ea_ref, s0_ref, ng_ref, o_ref, sfin_ref, st_sc,
                     *, bt, nsteps, dh, unroll):
    j = pl.program_id(1)
    w = qd_ref.shape[-1]
    heads = w // dh
    bones = _head_blocks(w, dh, F32)
    bones_b = bones.astype(BF16)

    @pl.when(j == 0)
    def _():
        def init(b, carry):
            s0 = s0_ref[b]
            st_sc[b] = (jnp.concatenate([s0] * heads, axis=1) * bones).T
            return carry
        lax.fori_loop(0, bt, init, 0)

    ng = ng_ref[...]

    def step(b, carry):
        st = st_sc[b]
        g = g_ref[b]
        o = oi_ref[b] + lax.dot_general(qd_ref[b].astype(BF16), st.astype(BF16), _NT, preferred_element_type=F32)
        upd = lax.dot_general(v_ref[b].astype(BF16), kd_ref[b].astype(BF16), _TN, preferred_element_type=F32)
        st_sc[b] = st * ea_ref[b][0:1, :] + upd * bones
        ms = _dot((o * o).astype(BF16), bones_b) * (1.0 / dh)
        o_ref[b] = o * lax.rsqrt(ms + EPS) * ng * (g * jax.nn.sigmoid(g))
        return carry
    lax.fori_loop(0, bt, step, 0, unroll=unroll)

    @pl.when(j == nsteps - 1)
    def _():
        def fin(b, carry):
            sbd = st_sc[b].T
            acc = sbd[:, 0:dh]
            for h in range(1, heads):
                acc = acc + sbd[:, h * dh:(h + 1) * dh]
            sfin_ref[b] = acc
            return carry
        lax.fori_loop(0, bt, fin, 0)


def _hgrn(q, lf, k, v, g, s0, norm_g, *, c, bt, dh):
    batch, t, w = q.shape
    rows = batch * t
    tm = math.gcd(rows, TM)
    flat = lambda a: a.reshape(rows, w)
    tile = pl.BlockSpec((tm, w), lambda i: (i, 0))
    oi, qd, kd, ea = pl.pallas_call(
        functools.partial(_hgrn_intra_kernel, c=c, dh=dh),
        grid=(rows // tm,),
        in_specs=[tile] * 4, out_specs=[tile] * 4,
        out_shape=[jax.ShapeDtypeStruct((rows, w), F32)] * 4,
        scratch_shapes=[pltpu.VMEM((tm, w), F32)],
        compiler_params=_cp("arbitrary"), name="hgrn_intra")(flat(q), flat(lf), flat(k), flat(v))
    nsteps = t // c
    tok = pl.BlockSpec((bt, c, w), lambda bi, j: (bi, j, 0))
    st = pl.BlockSpec((bt, w, dh), lambda bi, j: (bi, 0, 0))
    b3 = lambda a: a.reshape(batch, t, w)
    return pl.pallas_call(
        functools.partial(_hgrn_seq_kernel, bt=bt, nsteps=nsteps, dh=dh, unroll=8),
        grid=(batch // bt, nsteps),
        in_specs=[tok] * 6 + [st, _full((1, w))],
        out_specs=[tok, st],
        out_shape=[jax.ShapeDtypeStruct((batch, t, w), F32), jax.ShapeDtypeStruct((batch, w, dh), F32)],
        scratch_shapes=[pltpu.VMEM((bt, w, w), F32)],
        compiler_params=_cp("arbitrary", "arbitrary"), name="hgrn_seq")(
            b3(oi), b3(qd), b3(kd), v, g, b3(ea), s0, norm_g)


def _conv_kernel(glu_ref, buf_ref, w_ref, cb_ref, lg_ref, lb_ref, o_ref, nbuf_ref, ext_sc, *, bt, tt, taps, nt):
    t = pl.program_id(1)
    hist = CONV_HIST
    base = hist - (taps - 1)
    for bi in range(bt):
        @pl.when(t == 0)
        def _():
            ext_sc[bi, 0:hist, :] = buf_ref[bi]

        ext_sc[bi, hist:hist + tt, :] = glu_ref[bi]
        acc = w_ref[0:1, :] * ext_sc[bi, base:base + tt, :]
        for j in range(1, taps):
            acc = acc + w_ref[j:j + 1, :] * ext_sc[bi, base + j:base + j + tt, :]
        y = _ln(acc + cb_ref[...], lg_ref[...], lb_ref[...])
        o_ref[bi] = y * jax.nn.sigmoid(y)

        @pl.when(t == nt - 1)
        def _():
            nbuf_ref[bi] = ext_sc[bi, tt + base:tt + hist, :]

        ext_sc[bi, 0:hist, :] = ext_sc[bi, tt:tt + hist, :]


def _conv(glu, buf_padded, w, cb, lg, lb, *, tt, bt):
    batch, t, ch = glu.shape
    taps = w.shape[0]
    nt = t // tt
    return pl.pallas_call(
        functools.partial(_conv_kernel, bt=bt, tt=tt, taps=taps, nt=nt),
        grid=(batch // bt, nt),
        in_specs=[pl.BlockSpec((bt, tt, ch), lambda b, i: (b, i, 0)),
                  pl.BlockSpec((bt, CONV_HIST, ch), lambda b, i: (b, 0, 0)),
                  _full(w.shape), _full((1, ch)), _full((1, ch)), _full((1, ch))],
        out_specs=[pl.BlockSpec((bt, tt, ch), lambda b, i: (b, i, 0)),
                   pl.BlockSpec((bt, taps - 1, ch), lambda b, i: (b, 0, 0))],
        out_shape=[jax.ShapeDtypeStruct((batch, t, ch), F32), jax.ShapeDtypeStruct((batch, taps - 1, ch), F32)],
        scratch_shapes=[pltpu.VMEM((bt, CONV_HIST + tt, ch), F32)],
        compiler_params=_cp("arbitrary", "arbitrary"), name="conv")(glu, buf_padded, w, cb, lg, lb)


def _outproj_kernel(hp_ref, hs_ref, oa_ref, ob_ref, oc_ref, olat_ref, obs_ref, ocs_ref, wuv_ref, wout_ref, g_ref, b_ref,
                    wr_ref, br_ref, h1_ref, route_ref, routet_ref, mix_sc, *, ntp, alpha, heads, kv, dv, ne, ng):
    i = pl.program_id(0)
    da = heads * dv
    wb = ob_ref.shape[1]

    @pl.when(i < ntp)
    def _():
        mix_sc[...] = (alpha * hp_ref[...] + _dot(oa_ref[...], wout_ref[0:da, :])
                       + _dot(ob_ref[...].astype(BF16), wout_ref[da:da + wb, :])
                       + _dot(oc_ref[...].astype(BF16), wout_ref[da + wb:, :]))

    @pl.when(i >= ntp)
    def _():
        mix = (alpha * hs_ref[...] + _dot(obs_ref[...].astype(BF16), wout_ref[da:da + wb, :])
               + _dot(ocs_ref[...].astype(BF16), wout_ref[da + wb:, :]))
        for h in range(heads):
            oh = _dot(olat_ref[:, h * kv:(h + 1) * kv].astype(BF16), wuv_ref[h])
            mix = mix + _dot(oh.astype(BF16), wout_ref[h * dv:(h + 1) * dv, :])
        mix_sc[...] = mix

    half = mix_sc.shape[0] // 2
    for rs in (slice(0, half), slice(half, 2 * half)):
        _norm_and_route(mix_sc, h1_ref, route_ref, routet_ref, g_ref, b_ref, wr_ref, br_ref, rs, ne=ne, ng=ng)


def _norm_and_route(mix_sc, h1_ref, route_ref, routet_ref, g_ref, b_ref, wr_ref, br_ref, rs, *, ne, ng):
    h1 = _ln(mix_sc[rs, :], g_ref[...], b_ref[...])
    h1_ref[rs, :] = h1
    h_hi = h1.astype(BF16)
    h_lo = (h1 - h_hi.astype(F32)).astype(BF16)
    part = _dot(h_hi, wr_ref[...])
    logits = (part[:, 0:LANES] + part[:, LANES:] + _dot(h_lo, wr_ref[:, 0:LANES])
              + br_ref[...])
    tm = logits.shape[0]
    lane = lax.broadcasted_iota(I32, (tm, LANES), 1).astype(F32)
    big = jnp.float32(1e9)
    epg = ne // ng
    is_g = (lane >= ne) & (lane < ne + ng)
    gl = jnp.where(is_g, logits, -jnp.inf)
    gmax = jnp.max(gl, axis=1, keepdims=True)
    gidx = jnp.min(jnp.where(gl == gmax, lane, big), axis=1, keepdims=True) - ne
    p_g = 1.0 / jnp.sum(jnp.where(is_g, jnp.exp(logits - gmax), 0.0), axis=1, keepdims=True)
    lo = gidx * epg
    el = jnp.where((lane >= lo) & (lane < lo + epg), logits, -jnp.inf)
    v1 = jnp.max(el, axis=1, keepdims=True)
    e1 = jnp.min(jnp.where(el == v1, lane, big), axis=1, keepdims=True)
    el2 = jnp.where(lane == e1, -jnp.inf, el)
    v2 = jnp.max(el2, axis=1, keepdims=True)
    e2 = jnp.min(jnp.where(el2 == v2, lane, big), axis=1, keepdims=True)
    tt = jnp.exp(v2 - v1)
    w1 = p_g / (1.0 + tt)
    w2 = w1 * tt
    route = jnp.where(lane == 0, e1, jnp.where(lane == 1, e2, jnp.where(lane == 2, w1, jnp.where(lane == 3, w2, 0.0))))
    route_ref[rs, :] = route
    routet_ref[:, rs] = route.T[0:8, :]


def _outproj(h_p, h_s, oa_p, ob_p, oc_p, olat_s, ob_s, oc_s, lw, *, ntp, nts, alpha):
    d = h_p.shape[1]
    heads, kv, dv = lw["heads"], lw["kv"], lw["dv"]
    nt = ntp + nts
    t_all = nt * TM

    def pspec(n):
        return pl.BlockSpec((TM, n), lambda i: (jnp.minimum(i, ntp - 1), 0))

    def sspec(n):
        return pl.BlockSpec((TM, n), lambda i: (jnp.maximum(i - ntp, 0), 0))

    tok = pl.BlockSpec((TM, d), lambda i: (i, 0))
    return pl.pallas_call(
        functools.partial(_outproj_kernel, ntp=ntp, alpha=alpha, heads=heads, kv=kv, dv=dv, ne=lw["ne"], ng=lw["ng"]),
        grid=(nt,),
        in_specs=[pspec(d), sspec(d), pspec(oa_p.shape[1]), pspec(ob_p.shape[1]), pspec(oc_p.shape[1]),
                  sspec(olat_s.shape[1]), sspec(ob_s.shape[1]), sspec(oc_s.shape[1]),
                  _full(lw["w_uv_h"].shape), _full(lw["w_out"].shape), _full((1, d)), _full((1, d)),
                  _full(lw["w_r"].shape), _full((1, LANES))],
        out_specs=[tok, pl.BlockSpec((TM, LANES), lambda i: (i, 0)), pl.BlockSpec((8, TM), lambda i: (0, i))],
        out_shape=[jax.ShapeDtypeStruct((t_all, d), F32), jax.ShapeDtypeStruct((t_all, LANES), F32),
                   jax.ShapeDtypeStruct((8, t_all), F32)],
        scratch_shapes=[pltpu.VMEM((TM, d), F32)],
        compiler_params=_cp("arbitrary"), name="outproj")(
            h_p, h_s, oa_p, ob_p, oc_p, olat_s, ob_s, oc_s, lw["w_uv_h"], lw["w_out"], lw["ln1_g"], lw["ln1_b"],
            lw["w_r"], lw["b_r"])


def _plan_kernel(rt_ref, dest_ref, be_ref, meta_ref, seg_ref, cnt_sc, seg_sc, carry_sc, *, bm, ne, nbp):
    p = pl.program_id(0)
    first = pl.program_id(1) == 0
    tm = rt_ref.shape[1]
    e1, e2 = rt_ref[0:1, :], rt_ref[1:2, :]
    sub = lax.broadcasted_iota(I32, (LANES, tm), 0).astype(F32)
    oh1 = sub == e1
    oh2 = sub == e2
    ohs = jnp.where(oh1, 1.0, 0.0) + jnp.where(oh2, 1.0, 0.0)
    tile_cnt = jnp.sum(ohs, axis=1, keepdims=True)

    @pl.when((p == 0) & first)
    def _():
        cnt_sc[...] = jnp.zeros_like(cnt_sc)

    @pl.when(p == 0)
    def _():
        cnt_sc[...] = cnt_sc[...] + tile_cnt

    @pl.when((p == 1) & first)
    def _():
        cnt = cnt_sc[...]
        padded = jnp.ceil(cnt * (1.0 / bm)) * bm
        r = lax.broadcasted_iota(I32, (LANES, LANES), 0)
        cidx = lax.broadcasted_iota(I32, (LANES, LANES), 1)
        lstrict = jnp.where(cidx < r, 1.0, 0.0).astype(F32)
        seg_start = _dot_hi(lstrict, jnp.broadcast_to(padded, (LANES, LANES)))[:, 0:1]
        seg_sc[...] = seg_start
        carry_sc[...] = jnp.zeros_like(carry_sc)
        seg_end = seg_start + padded
        jl = lax.broadcasted_iota(I32, (LANES, nbp), 1).astype(F32) * bm
        be = jnp.sum(jnp.where(seg_end <= jl, 1.0, 0.0), axis=0, keepdims=True)
        be_ref[...] = jnp.minimum(be, ne - 1).astype(I32)
        total = jnp.max(seg_end, axis=0, keepdims=True)
        meta_ref[...] = jnp.broadcast_to(total * (1.0 / bm), (1, LANES)).astype(I32)
        eye = r == cidx
        seg_ref[0:1, :] = jnp.sum(jnp.where(eye, seg_start, 0.0), axis=0, keepdims=True).astype(I32)
        seg_ref[1:2, :] = jnp.sum(jnp.where(eye, seg_end, 0.0), axis=0, keepdims=True).astype(I32)

    @pl.when(p == 1)
    def _():
        rr = lax.broadcasted_iota(I32, (tm, tm), 0)
        ccol = lax.broadcasted_iota(I32, (tm, tm), 1)
        ustrict = jnp.where(rr < ccol, 1.0, 0.0).astype(BF16)
        base = _dot(ohs.astype(BF16), ustrict) + carry_sc[...] + seg_sc[...]
        d1 = jnp.sum(jnp.where(oh1, base, 0.0), axis=0, keepdims=True)
        d2 = jnp.sum(jnp.where(oh2, base, 0.0), axis=0, keepdims=True)
        dest_ref[0:1, :] = d1.astype(I32)
        dest_ref[1:2, :] = d2.astype(I32)
        carry_sc[...] = carry_sc[...] + tile_cnt


def _plan(route_t, *, ne, nb_max):
    t_all = route_t.shape[1]
    tm = 512 if t_all % 512 == 0 else TM
    nt = t_all // tm
    nbp = pl.cdiv(nb_max, LANES) * LANES
    return pl.pallas_call(
        functools.partial(_plan_kernel, bm=MOE_BM, ne=ne, nbp=nbp),
        grid=(2, nt),
        in_specs=[pl.BlockSpec((8, tm), lambda p, i: (0, i))],
        out_specs=[pl.BlockSpec((2, tm), lambda p, i: (0, i * p)), _full((1, nbp)), _full((1, LANES)),
                   _full((2, LANES))],
        out_shape=[jax.ShapeDtypeStruct((2, t_all), I32), jax.ShapeDtypeStruct((1, nbp), I32),
                   jax.ShapeDtypeStruct((1, LANES), I32), jax.ShapeDtypeStruct((2, LANES), I32)],
        scratch_shapes=[pltpu.VMEM((LANES, 1), F32)] * 3,
        compiler_params=_cp("arbitrary", "arbitrary"), name="moe_plan")(route_t)


def _scatter_kernel(d1_ref, d2_ref, seg_ref, meta_ref, h_ref, rows_hbm, zbuf, sem, zsem, *, tm, ne, bm, nb):
    @pl.when(pl.program_id(0) == 0)
    def _():
        zbuf[...] = jnp.zeros_like(zbuf)
        for e in range(ne):
            @pl.when(seg_ref[1, e] > seg_ref[0, e])
            def _():
                last = pl.multiple_of(seg_ref[1, e] - bm, bm)
                pltpu.make_async_copy(zbuf, rows_hbm.at[pl.ds(last, bm)], zsem).start()

        def ztail(jb, carry):
            pltpu.make_async_copy(zbuf, rows_hbm.at[pl.ds(pl.multiple_of(jb * bm, bm), bm)], zsem).start()
            return carry
        lax.fori_loop(meta_ref[0, 0], nb, ztail, 0)
        for e in range(ne):
            @pl.when(seg_ref[1, e] > seg_ref[0, e])
            def _():
                pltpu.make_async_copy(zbuf, rows_hbm.at[pl.ds(0, bm)], zsem).wait()

        def zwait(jb, carry):
            pltpu.make_async_copy(zbuf, rows_hbm.at[pl.ds(0, bm)], zsem).wait()
            return carry
        lax.fori_loop(meta_ref[0, 0], nb, zwait, 0)

    for r in range(tm):
        src = h_ref.at[pl.ds(r, 1)]
        pltpu.make_async_copy(src, rows_hbm.at[pl.ds(d1_ref[0, 0, r], 1)], sem.at[0]).start(priority=0)
        pltpu.make_async_copy(src, rows_hbm.at[pl.ds(d2_ref[0, 0, r], 1)], sem.at[1]).start(priority=1)
    pltpu.make_async_copy(h_ref, rows_hbm.at[pl.ds(0, tm)], sem.at[0]).wait()
    pltpu.make_async_copy(h_ref, rows_hbm.at[pl.ds(0, tm)], sem.at[1]).wait()


def _scatter(h1, d1, d2, seg, meta, nrows, *, ne):
    t_all, d = h1.shape
    tm = d1.shape[2]
    dspec = pl.BlockSpec((1, 1, tm), lambda i: (i, 0, 0), memory_space=pltpu.SMEM)
    smem = pl.BlockSpec(memory_space=pltpu.SMEM)
    return pl.pallas_call(
        functools.partial(_scatter_kernel, tm=tm, ne=ne, bm=MOE_BM, nb=nrows // MOE_BM),
        grid=(t_all // tm,),
        in_specs=[dspec, dspec, smem, smem, pl.BlockSpec((tm, d), lambda i: (i, 0))],
        out_specs=pl.BlockSpec(memory_space=pl.ANY),
        out_shape=jax.ShapeDtypeStruct((nrows, d), F32),
        scratch_shapes=[pltpu.VMEM((MOE_BM, d), F32), pltpu.SemaphoreType.DMA((2,)), pltpu.SemaphoreType.DMA(())],
        compiler_params=_cp("arbitrary"), name="moe_scatter")(d1, d2, seg, meta, h1)


def _ffn_kernel(be_ref, na_ref, x_ref, wg_ref, wu_ref, wd_ref, y_ref, wgb, wub, wdb):
    i = pl.program_id(0)

    @pl.when(i < na_ref[0])
    def _():
        @pl.when((i == 0) | (be_ref[i] != be_ref[jnp.maximum(i - 1, 0)]))
        def _():
            wgb[...] = wg_ref[...].astype(BF16)
            wub[...] = wu_ref[...].astype(BF16)
            wdb[...] = wd_ref[...].astype(BF16)

        half = x_ref.shape[0] // 2
        for rs in (slice(0, half), slice(half, 2 * half)):
            x = x_ref[rs, :].astype(BF16)
            g = _dot(x, wgb[...])
            u = _dot(x, wub[...])
            hdn = (g * jax.nn.sigmoid(g) * u).astype(BF16)
            y_ref[rs, :] = _dot(hdn, wdb[...])

    @pl.when(i >= na_ref[0])
    def _():
        y_ref[...] = jnp.zeros_like(y_ref)


def _ffn(rows, be, nact, w_gate, w_up, w_down, layer):
    nrows, d = rows.shape
    ff = w_gate.shape[3]
    nb = nrows // MOE_BM

    def blk(i, be_r, na_r):
        return jnp.minimum(i, na_r[0] - 1)

    def wspec(a, b):
        return pl.BlockSpec((None, None, a, b), lambda i, be_r, na_r: (layer, be_r[blk(i, be_r, na_r)], 0, 0))

    row_spec = pl.BlockSpec((MOE_BM, d), lambda i, be_r, na_r: (blk(i, be_r, na_r), 0))
    return pl.pallas_call(
        _ffn_kernel,
        grid_spec=pltpu.PrefetchScalarGridSpec(
            num_scalar_prefetch=2, grid=(nb,),
            in_specs=[row_spec, wspec(d, ff), wspec(d, ff), wspec(ff, d)],
            out_specs=pl.BlockSpec((MOE_BM, d), lambda i, be_r, na_r: (i, 0)),
            scratch_shapes=[pltpu.VMEM((d, ff), BF16), pltpu.VMEM((d, ff), BF16), pltpu.VMEM((ff, d), BF16)]),
        out_shape=jax.ShapeDtypeStruct((nrows, d), F32),
        compiler_params=_cp("arbitrary"), name="moe_ffn")(be, nact, rows, w_gate, w_up, w_down)


def _combine_kernel(d1_ref, d2_ref, d1n_ref, d2n_ref, h1_ref, route_ref, yb_hbm, g_ref, b_ref, outp_ref, outs_ref,
                    buf1, buf2, sem, *, tm, nt, ntp, alpha):
    i = pl.program_id(0)
    slot = lax.rem(i, 2)

    def gather(da_ref, db_ref, s):
        for r in range(tm):
            pltpu.make_async_copy(yb_hbm.at[pl.ds(da_ref[0, 0, r], 1)], buf1.at[s, pl.ds(r, 1)],
                                  sem.at[0, s]).start(priority=0)
            pltpu.make_async_copy(yb_hbm.at[pl.ds(db_ref[0, 0, r], 1)], buf2.at[s, pl.ds(r, 1)],
                                  sem.at[1, s]).start(priority=1)

    @pl.when(i == 0)
    def _():
        gather(d1_ref, d2_ref, 0)

    @pl.when(i + 1 < nt)
    def _():
        gather(d1n_ref, d2n_ref, 1 - slot)

    pltpu.make_async_copy(yb_hbm.at[pl.ds(0, tm)], buf1.at[slot], sem.at[0, slot]).wait()
    pltpu.make_async_copy(yb_hbm.at[pl.ds(0, tm)], buf2.at[slot], sem.at[1, slot]).wait()
    moe = route_ref[:, 2:3] * buf1[slot] + route_ref[:, 3:4] * buf2[slot]
    out = _ln(alpha * h1_ref[...] + moe, g_ref[...], b_ref[...])

    @pl.when(i < ntp)
    def _():
        outp_ref[...] = out

    @pl.when(i >= ntp)
    def _():
        outs_ref[...] = out


def _combine(h1, route, yb, d1, d2, g, b, *, ntp, alpha):
    t_all, d = h1.shape
    tm = d1.shape[2]
    nt = t_all // tm
    dspec = pl.BlockSpec((1, 1, tm), lambda i: (i, 0, 0), memory_space=pltpu.SMEM)
    dnext = pl.BlockSpec((1, 1, tm), lambda i: (jnp.minimum(i + 1, nt - 1), 0, 0), memory_space=pltpu.SMEM)
    tok = pl.BlockSpec((tm, d), lambda i: (i, 0))
    return pl.pallas_call(
        functools.partial(_combine_kernel, tm=tm, nt=nt, ntp=ntp, alpha=alpha),
        grid=(nt,),
        in_specs=[dspec, dspec, dnext, dnext, tok, pl.BlockSpec((tm, LANES), lambda i: (i, 0)),
                  pl.BlockSpec(memory_space=pl.ANY), _full((1, d)), _full((1, d))],
        out_specs=[pl.BlockSpec((tm, d), lambda i: (jnp.minimum(i, ntp - 1), 0)),
                   pl.BlockSpec((tm, d), lambda i: (jnp.maximum(i - ntp, 0), 0))],
        out_shape=[jax.ShapeDtypeStruct((ntp * tm, d), F32), jax.ShapeDtypeStruct(((nt - ntp) * tm, d), F32)],
        scratch_shapes=[pltpu.VMEM((2, tm, d), F32), pltpu.VMEM((2, tm, d), F32), pltpu.SemaphoreType.DMA((2, 2))],
        compiler_params=_cp("arbitrary"), name="moe_combine")(d1, d2, d1, d2, h1, route, yb, g, b)


def _rope_tables(pos, dr, scale):
    half = dr // 2
    inv = ROPE_BASE ** (-jnp.arange(half, dtype=F32) / half)
    ang = pos.astype(F32)[:, None] * inv[None, :]
    cos, sin = jnp.cos(ang), jnp.sin(ang)
    n = pos.shape[0]
    z64, z32, o64 = jnp.zeros((n, 64), F32), jnp.zeros((n, 32), F32), jnp.ones((n, 64), F32)
    cq = jnp.concatenate([o64, cos, cos, z32], axis=1) * scale
    sq = jnp.concatenate([z64, -sin, sin, z32], axis=1) * scale
    ck = jnp.concatenate([z64, cos, cos, z32], axis=1)
    sk = jnp.concatenate([z64, -sin, sin, z32], axis=1)
    return cq, sq, ck, sk


def _prep_layer(l, p):
    w_in = p["w_in"][l]
    d = w_in.shape[0]
    ql, kv = p["mla_q_norm"].shape[1], p["mla_kv_norm"].shape[1]
    heads, dn = p["mla_w_uk"].shape[2], p["mla_w_uk"].shape[3]
    dv = p["mla_w_uv"].shape[3]
    dr = p["mla_w_uq"].shape[2] // heads - dn
    assert dn == 64 and dr == 32 and dv == 64 and ql % LANES == 0 and kv % LANES == 0
    hw = p["hgrn_lb_logits"].shape[1]
    ch = p["conv_b"].shape[1]
    assert hw == 256 and ch == 256
    o = ql + kv
    kr = w_in[:, o:o + dr]
    x1, x2 = kr[:, :dr // 2], kr[:, dr // 2:]
    seg = jnp.concatenate([jnp.zeros((d, 64), F32), x1, x2, x2, x1], axis=1)
    w_in_p = jnp.concatenate([w_in[:, :o], seg, w_in[:, o + dr:]], axis=1).astype(BF16)
    uq = p["mla_w_uq"][l].reshape(ql, heads, dn + dr)
    u1, u2 = uq[:, :, dn:dn + dr // 2], uq[:, :, dn + dr // 2:]
    w_uq = jnp.concatenate([uq[:, :, :dn], u1, u2, u2, u1], axis=2).reshape(ql, heads * LANES).astype(BF16)
    uk = p["mla_w_uk"][l]
    w_uk = jnp.concatenate([uk, jnp.zeros_like(uk)], axis=2).reshape(kv, heads * LANES).astype(BF16)
    w_ukt = jnp.transpose(uk, (1, 2, 0)).astype(BF16)
    uv = p["mla_w_uv"][l]
    lb = p["lb_all"][l]
    lbc = jnp.zeros((8, hw), F32).at[0].set(jnp.log(lb)).at[1].set(jnp.log1p(-lb)).at[2].set(1.0 - lb)
    ne, ng = p["router_e_w"].shape[2], p["router_g_w"].shape[2]
    w_r = jnp.concatenate([p["router_e_w"][l], p["router_g_w"][l], jnp.zeros((d, LANES - ne - ng), F32)], axis=1)
    b_r = jnp.concatenate([p["router_e_b"][l], p["router_g_b"][l], jnp.zeros((LANES - ne - ng,), F32)])[None, :]
    w_r_hi = w_r.astype(BF16)
    w_r = jnp.concatenate([w_r_hi, (w_r - w_r_hi.astype(F32)).astype(BF16)], axis=1)
    return dict(
        ql=ql, kv=kv, heads=heads, dn=dn, dv=dv, dr=dr, ne=ne, ng=ng,
        w_in=w_in_p, q_norm=p["mla_q_norm"][l][None, :], w_uq=w_uq, kv_norm=p["mla_kv_norm"][l][None, :],
        w_uk=w_uk, w_ukt=w_ukt, w_uv=uv.reshape(kv, heads * dv).astype(BF16),
        w_uv_h=jnp.transpose(uv, (1, 0, 2)).astype(BF16), lbc=lbc,
        hgrn_norm=jnp.tile(p["hgrn_norm"][l], hw // p["hgrn_norm"].shape[1])[None, :],
        conv_w=p["conv_w"][l], conv_b=p["conv_b"][l][None, :],
        conv_ln_g=p["conv_ln_g"][l][None, :], conv_ln_b=p["conv_ln_b"][l][None, :],
        w_out=p["w_out"][l].astype(BF16), ln1_g=p["ln1_g"][l][None, :], ln1_b=p["ln1_b"][l][None, :],
        ln2_g=p["ln2_g"][l][None, :], ln2_b=p["ln2_b"][l][None, :], w_r=w_r, b_r=b_r)


def kernel(x_prompt, x_sample, cache_ckv, cache_krope, state_hgrn, state_conv, page_table, w_in, mla_q_norm, mla_w_uq, mla_kv_norm, mla_w_uk, mla_w_uv, hgrn_lb_logits, hgrn_norm, conv_w, conv_b, conv_ln_g, conv_ln_b, w_out, ln1_g, ln1_b, ln2_g, ln2_b, router_g_w, router_g_b, router_e_w, router_e_b, exp_w_gate, exp_w_up, exp_w_down):
    batch, seq, d = x_prompt.shape
    nseq, tnew, _ = x_sample.shape
    depth = w_in.shape[0]
    t_p, t_s = batch * seq, nseq * tnew
    t_all = t_p + t_s
    assert t_p % TM == 0 and t_s % TM == 0 and seq % TM == 0 and seq % HGRN_C == 0
    ntp, nts = t_p // TM, t_s // TM
    past_len = page_table.shape[1] * cache_ckv.shape[2]
    alpha = (2 * depth) ** 0.25
    hb, dkb = state_hgrn.shape[2], state_hgrn.shape[3]
    taps = conv_w.shape[1]
    assert taps - 1 <= CONV_HIST

    lbp = jax.nn.softmax(hgrn_lb_logits.astype(F32), axis=0)
    lbcs = jnp.cumsum(lbp, axis=0)
    params = dict(w_in=w_in, mla_q_norm=mla_q_norm, mla_w_uq=mla_w_uq, mla_kv_norm=mla_kv_norm, mla_w_uk=mla_w_uk,
                  mla_w_uv=mla_w_uv, hgrn_lb_logits=hgrn_lb_logits, lb_all=lbcs - lbcs[0:1], hgrn_norm=hgrn_norm,
                  conv_w=conv_w, conv_b=conv_b, conv_ln_g=conv_ln_g, conv_ln_b=conv_ln_b, w_out=w_out,
                  ln1_g=ln1_g, ln1_b=ln1_b, ln2_g=ln2_g, ln2_b=ln2_b, router_g_w=router_g_w, router_g_b=router_g_b,
                  router_e_w=router_e_w, router_e_b=router_e_b)
    dn = mla_w_uk.shape[3]
    dr = cache_krope.shape[3]
    scale = (dn + dr) ** -0.5 * math.log2(math.e)
    tabs_p = _rope_tables(jnp.arange(seq), dr, scale)
    tabs_s = _rope_tables(jnp.tile(past_len + jnp.arange(tnew), nseq), dr, scale)

    ne = router_e_w.shape[2]
    n_assign = 2 * t_all
    nb_max = (n_assign + ne * (MOE_BM - 1) + MOE_BM - 1) // MOE_BM
    nrows = nb_max * MOE_BM

    h_p, h_s = x_prompt.reshape(t_p, d), x_sample.reshape(t_s, d)
    outs = {k: [] for k in ("ckv_p", "kr_p", "sh_p", "sc_p", "ckv_s", "kr_s", "sh_s", "sc_s")}
    hw = hb * dkb
    tpad = 8
    bt_s = math.gcd(nseq, 16)
    for l in range(depth):
        lw = _prep_layer(l, params)
        heads, kv, dv = lw["heads"], lw["kv"], lw["dv"]
        (qatt, katt, vatt, ckv_p, kr_p, hq, hlf, hk, hv, hg, glu) = _inproj(
            h_p, lw, tabs_p, tab_period=seq // TM, sample=False)
        oa_p = _attn_prompt(qatt, katt, vatt, batch=batch, seq=seq, heads=heads, dv=dv)
        r3 = lambda a: a.reshape(batch, seq, a.shape[1])
        ob_p, s_p = _hgrn(r3(hq), r3(hlf), r3(hk), r3(hv), r3(hg), jnp.zeros((batch, hw, dkb), F32), lw["hgrn_norm"],
                          c=HGRN_C, bt=batch, dh=dkb)
        oc_p, buf_p = _conv(r3(glu), jnp.zeros((batch, CONV_HIST, glu.shape[1]), F32), lw["conv_w"], lw["conv_b"],
                            lw["conv_ln_g"], lw["conv_ln_b"], tt=TM, bt=1)
        (qatt_s, qlat_s, ckv_s, kr_s, hq, hlf, hk, hv, hg, glu_s) = _inproj(
            h_s, lw, tabs_s, tab_period=nts, sample=True)
        olat_s = _attn_sample(page_table, qlat_s, qatt_s, ckv_s, kr_s, cache_ckv, cache_krope, l,
                              heads=heads, tnew=tnew, dn=lw["dn"])
        r3s = lambda a: jnp.pad(a.reshape(nseq, tnew, a.shape[1]), ((0, 0), (0, tpad - tnew), (0, 0)))
        ob_s, s_s = _hgrn(r3s(hq), r3s(hlf), r3s(hk), r3s(hv), r3s(hg), state_hgrn[l].reshape(nseq, hw, dkb),
                          lw["hgrn_norm"], c=tpad, bt=bt_s, dh=dkb)
        ob_s = ob_s[:, :tnew].reshape(t_s, hw)
        bufpad = jnp.pad(state_conv[l], ((0, 0), (CONV_HIST - (taps - 1), 0), (0, 0)))
        oc_s, buf_s = _conv(glu_s.reshape(nseq, tnew, -1), bufpad, lw["conv_w"], lw["conv_b"],
                            lw["conv_ln_g"], lw["conv_ln_b"], tt=tnew, bt=math.gcd(nseq, 8))
        h1, route, route_t = _outproj(h_p, h_s, oa_p, ob_p.reshape(t_p, hw), oc_p.reshape(t_p, -1), olat_s, ob_s,
                                      oc_s.reshape(t_s, -1), lw, ntp=ntp, nts=nts, alpha=alpha)
        dest, be, meta, seg = _plan(route_t, ne=ne, nb_max=nb_max)
        d1 = dest[0].reshape(t_all // TM, 1, TM)
        d2 = dest[1].reshape(t_all // TM, 1, TM)
        rows = _scatter(h1, d1, d2, seg, meta, nrows, ne=ne)
        yb = _ffn(rows, be.reshape(-1), meta[0, 0:1], exp_w_gate, exp_w_up, exp_w_down, l)
        h_p, h_s = _combine(h1, route, yb, d1, d2, lw["ln2_g"], lw["ln2_b"], ntp=ntp, alpha=alpha)

        outs["ckv_p"].append(ckv_p.reshape(batch, seq, kv))
        outs["kr_p"].append(kr_p.reshape(batch, seq, dr))
        outs["sh_p"].append(s_p.reshape(batch, hb, dkb, dkb))
        outs["sc_p"].append(buf_p)
        outs["ckv_s"].append(ckv_s.reshape(nseq, tnew, kv))
        outs["kr_s"].append(kr_s.reshape(nseq, tnew, dr))
        outs["sh_s"].append(s_s.reshape(nseq, hb, dkb, dkb))
        outs["sc_s"].append(buf_s)
    st = lambda k: jnp.stack(outs[k])
    return (h_p.reshape(batch, seq, d), h_s.reshape(nseq, tnew, d),
            st("ckv_p"), st("kr_p"), st("sh_p"), st("sc_p"), st("ckv_s"), st("kr_s"), st("sh_s"), st("sc_s"))
```
